```python
import jax, jax.numpy as jnp
from jax import lax
import numpy as np

D_MODEL = 2048
BATCH = 32
SEQ = 256
DEPTH = 4
DEC_BATCH = 2
DEC_SEQ = 2048
PAST_LEN = 256

GRID_W = 64
HEAD_DIM = 128
POOL_GROUPS = 4
POOL_WIDTH = D_MODEL // 4
POOL_WINDOWS = (2, 4, 8, 16)
B_Q_HEADS = (D_MODEL - POOL_WIDTH) // HEAD_DIM
B_KV_HEADS = B_Q_HEADS // 3
D_WIDTH = D_MODEL // 4
C_HEADS = (D_MODEL - D_WIDTH) // HEAD_DIM
CONV_WIDTH = 3
NA_KH = 8
NA_KW = 16
FFN_HIDDEN = -(-8 * D_MODEL // (3 * 256)) * 256
AB_IN = POOL_WIDTH + (B_Q_HEADS + 2 * B_KV_HEADS) * HEAD_DIM
CD_IN = 3 * C_HEADS * HEAD_DIM + 3 * D_WIDTH
MIX_OUT = D_MODEL
N_EVEN = (DEPTH + 1) // 2
N_ODD = DEPTH // 2
ALPHA = (2 * DEPTH) ** 0.25
BETA = (8 * DEPTH) ** -0.25
ROPE_THETA = 10000.0
LN_EPS = 1e-5
RMS_EPS = 1e-6
Q_BLOCK = 128

kernel_name = 'hybrid_flow_pool_gqa_natten_shortconv_step'


def layer_norm(x, g, b):
    xf = x.astype(jnp.float32)
    mu = jnp.mean(xf, axis=-1, keepdims=True)
    var = jnp.mean(jnp.square(xf - mu), axis=-1, keepdims=True)
    return ((xf - mu) * lax.rsqrt(var + LN_EPS) * g + b).astype(x.dtype)


def rms_norm_heads(x, g):
    xf = x.astype(jnp.float32)
    return (xf * lax.rsqrt(jnp.mean(xf * xf, axis=-1, keepdims=True) + RMS_EPS) * g).astype(x.dtype)


def adaln(cond, w_ada, b_ada):
    m = (jax.nn.silu(cond) @ w_ada + b_ada)[:, None, :]
    return jnp.split(m, 6, axis=-1)


def modulate(x, shift, scale):
    return x * (1 + scale) + shift


def axial_rope_tables(n_tokens):
    t = jnp.arange(n_tokens)
    row = (t // GRID_W).astype(jnp.float32)
    col = (t % GRID_W).astype(jnp.float32)
    half = HEAD_DIM // 2
    inv = 1.0 / (ROPE_THETA ** (jnp.arange(0, half, 2, dtype=jnp.float32) / half))
    ang = jnp.concatenate([row[:, None] * inv, col[:, None] * inv], axis=-1)
    return jnp.cos(ang), jnp.sin(ang)


def apply_rope(x, cos, sin):
    xf = x.astype(jnp.float32)
    half = HEAD_DIM // 2
    x1, x2 = xf[..., :half], xf[..., half:]
    c = cos[None, :, None, :]
    s = sin[None, :, None, :]
    return jnp.concatenate([x1 * c - x2 * s, x2 * c + x1 * s], axis=-1).astype(x.dtype)


def blocked_attention(q, k, v):
    b, L, hq, d = q.shape
    hkv = k.shape[2]
    g = hq // hkv
    nb = L // Q_BLOCK
    qb = jnp.moveaxis(q.reshape(b, nb, Q_BLOCK, hkv, g, d), 1, 0)
    scale = d ** -0.5

    def one_block(qblk):
        s = jnp.einsum('bqhgd,bkhd->bhgqk', qblk, k, preferred_element_type=jnp.float32) * scale
        p = jax.nn.softmax(s, axis=-1).astype(v.dtype)
        return jnp.einsum('bhgqk,bkhd->bqhgd', p, v)

    o = lax.map(one_block, qb)
    return jnp.moveaxis(o, 0, 1).reshape(b, L, hq, d)


def neighborhood_attention(q, k, v, ctx_k, ctx_v, rpb):
    b, L, h, d = q.shape
    rows = L // GRID_W
    kh = min(NA_KH, rows)
    kw = NA_KW
    r = jnp.arange(rows)
    row_start = jnp.clip(r - kh // 2, 0, rows - kh)
    row_idx = row_start[:, None] + jnp.arange(kh)[None, :]
    nk = kh * GRID_W
    qg = q.reshape(b, rows, GRID_W, h, d)
    band_k = k.reshape(b, rows, GRID_W, h, d)[:, row_idx].reshape(b, rows, nk, h, d)
    band_v = v.reshape(b, rows, GRID_W, h, d)[:, row_idx].reshape(b, rows, nk, h, d)
    scale = d ** -0.5
    s_nb = jnp.einsum('brqhd,brkhd->bhrqk', qg, band_k, preferred_element_type=jnp.float32) * scale
    col = jnp.arange(GRID_W)
    col_start = jnp.clip(col - kw // 2, 0, GRID_W - kw)
    col_ok = (col[None, :] >= col_start[:, None]) & (col[None, :] < col_start[:, None] + kw)
    mask = jnp.broadcast_to(col_ok[:, None, :], (GRID_W, kh, GRID_W)).reshape(GRID_W, nk)
    dr_idx = (row_idx - r[:, None]) + NA_KH - 1
    dc_idx = jnp.clip(col[None, :] - col[:, None], -(kw - 1), kw - 1) + kw - 1
    bias = rpb[:, dr_idx[:, None, :, None], dc_idx[None, :, None, :]].reshape(h, rows, GRID_W, nk)
    s_nb = jnp.where(mask, s_nb + bias[None].astype(jnp.float32), -jnp.inf)
    s_ctx = jnp.einsum('brqhd,bkhd->bhrqk', qg, ctx_k, preferred_element_type=jnp.float32) * scale
    p = jax.nn.softmax(jnp.concatenate([s_nb, s_ctx], axis=-1), axis=-1).astype(v.dtype)
    o = (jnp.einsum('bhrqk,brkhd->brqhd', p[..., :nk], band_v)
         + jnp.einsum('bhrqk,bkhd->brqhd', p[..., nk:], ctx_v))
    return o.reshape(b, L, h, d)


def multiscale_pool(u, pool_w, pool_scale):
    b, L, _ = u.shape
    cw = POOL_WIDTH // POOL_GROUPS
    ug = u.reshape(b, L, POOL_GROUPS, cw)
    csum = jnp.cumsum(ug.astype(jnp.float32), axis=1)
    csum = jnp.concatenate([jnp.zeros_like(csum[:, :1]), csum], axis=1)
    t = jnp.arange(L)[:, None]
    w = jnp.array(POOL_WINDOWS, dtype=jnp.int32)[None, :]
    lo = jnp.clip(t - w // 2, 0, L - 1)
    hi = jnp.clip(t - w // 2 + w - 1, 0, L - 1)
    gidx = jnp.arange(POOL_GROUPS)[None, :]
    window_sum = csum[:, hi + 1, gidx] - csum[:, lo, gidx]
    mean = window_sum / (hi - lo + 1).astype(jnp.float32)[None, :, :, None]
    pooled = (mean - ug.astype(jnp.float32)).astype(u.dtype)
    y = jnp.einsum('blgc,gcd->blgd', pooled, pool_w)
    return y.reshape(b, L, POOL_WIDTH) * pool_scale


def short_gated_conv(xin, gb, gc, conv_w):
    u = gc * xin
    up = jnp.pad(u, ((0, 0), (1, 1), (0, 0)))
    y = conv_w[0] * up[:, :-2] + conv_w[1] * up[:, 1:-1] + conv_w[2] * up[:, 2:]
    return gb * y


def mixer_ab(h, w_in, w_out, pool_w, pool_scale, q_g, k_g, rope=None, ctx_k=None, ctx_v=None):
    b, L, _ = h.shape
    proj = h @ w_in
    s1 = POOL_WIDTH
    s2 = s1 + B_Q_HEADS * HEAD_DIM
    s3 = s2 + B_KV_HEADS * HEAD_DIM
    u, q, k, v = jnp.split(proj, [s1, s2, s3], axis=-1)
    q = rms_norm_heads(q.reshape(b, L, B_Q_HEADS, HEAD_DIM), q_g)
    k = rms_norm_heads(k.reshape(b, L, B_KV_HEADS, HEAD_DIM), k_g)
    v = v.reshape(b, L, B_KV_HEADS, HEAD_DIM)
    pool_out = multiscale_pool(u, pool_w, pool_scale)
    if ctx_k is None:
        attn = blocked_attention(q, k, v)
    else:
        cos, sin = rope
        qr = apply_rope(q, cos, sin)
        kr = apply_rope(k, cos, sin)
        attn = blocked_attention(qr, jnp.concatenate([kr, ctx_k], axis=1), jnp.concatenate([v, ctx_v], axis=1))
    out = jnp.concatenate([pool_out, attn.reshape(b, L, B_Q_HEADS * HEAD_DIM)], axis=-1) @ w_out
    return out, k, v


def mixer_cd(h, w_in, w_out, rpb, conv_w, ctx_k=None, ctx_v=None):
    b, L, _ = h.shape
    cw = C_HEADS * HEAD_DIM
    proj = h @ w_in
    q, k, v, xin, gb, gc = jnp.split(proj, [cw, 2 * cw, 3 * cw, 3 * cw + D_WIDTH, 3 * cw + 2 * D_WIDTH], axis=-1)
    q = q.reshape(b, L, C_HEADS, HEAD_DIM)
    k = k.reshape(b, L, C_HEADS, HEAD_DIM)
    v = v.reshape(b, L, C_HEADS, HEAD_DIM)
    if ctx_k is None:
        attn = blocked_attention(q, k, v)
    else:
        attn = neighborhood_attention(q, k, v, ctx_k, ctx_v, rpb)
    conv_out = short_gated_conv(xin, gb, gc, conv_w)
    out = jnp.concatenate([attn.reshape(b, L, cw), conv_out], axis=-1) @ w_out
    return out, k, v


def swiglu(h, wg, wu, wd):
    return (jax.nn.silu(h @ wg) * (h @ wu)) @ wd


def setup_inputs(seed: int = 0) -> dict:
    key = jax.random.key(seed)
    ks = jax.random.split(key, 32)

    def nrm(k, shape, scale=1.0):
        return jax.random.normal(k, shape, jnp.float32) * scale

    kvb = (DEC_BATCH, PAST_LEN, B_KV_HEADS, HEAD_DIM)
    kvc = (DEC_BATCH, PAST_LEN, C_HEADS, HEAD_DIM)
    return {
        'x_prompt': nrm(ks[0], (BATCH, SEQ, D_MODEL)),
        'x_sample': nrm(ks[1], (DEC_BATCH, DEC_SEQ, D_MODEL)),
        'cache_k_l0': nrm(ks[2], kvb),
        'cache_v_l0': nrm(ks[3], kvb),
        'cache_k_l1': nrm(ks[4], kvc),
        'cache_v_l1': nrm(ks[5], kvc),
        'cache_k_l2': nrm(ks[6], kvb),
        'cache_v_l2': nrm(ks[7], kvb),
        'cache_k_l3': nrm(ks[8], kvc),
        'cache_v_l3': nrm(ks[9], kvc),
        'c': nrm(ks[10], (DEC_BATCH, D_MODEL)),
        'c_ctx': nrm(ks[11], (D_MODEL,)),
        'w_ada': nrm(ks[12], (DEPTH, D_MODEL, 6 * D_MODEL), 0.5 * D_MODEL ** -0.5),
        'b_ada': nrm(ks[13], (DEPTH, 6 * D_MODEL), 0.01),
        'ln1_g': 1.0 + nrm(ks[14], (DEPTH, D_MODEL), 0.05),
        'ln1_b': nrm(ks[15], (DEPTH, D_MODEL), 0.02),
        'ln2_g': 1.0 + nrm(ks[16], (DEPTH, D_MODEL), 0.05),
        'ln2_b': nrm(ks[17], (DEPTH, D_MODEL), 0.02),
        'w_in_ab': nrm(ks[18], (N_EVEN, D_MODEL, AB_IN), D_MODEL ** -0.5),
        'w_out_ab': nrm(ks[19], (N_EVEN, MIX_OUT, D_MODEL), BETA * MIX_OUT ** -0.5),
        'pool_w': nrm(ks[20], (N_EVEN, POOL_GROUPS, POOL_WIDTH // POOL_GROUPS, POOL_WIDTH // POOL_GROUPS), (POOL_WIDTH // POOL_GROUPS) ** -0.5),
        'pool_scale': 1.0 + nrm(ks[21], (N_EVEN, POOL_WIDTH), 0.1),
        'q_norm_g': 1.0 + nrm(ks[22], (N_EVEN, HEAD_DIM), 0.05),
        'k_norm_g': 1.0 + nrm(ks[23], (N_EVEN, HEAD_DIM), 0.05),
        'w_in_cd': nrm(ks[24], (N_ODD, D_MODEL, CD_IN), D_MODEL ** -0.5),
        'w_out_cd': nrm(ks[25], (N_ODD, MIX_OUT, D_MODEL), BETA * MIX_OUT ** -0.5),
        'na_rpb': nrm(ks[26], (N_ODD, C_HEADS, 2 * NA_KH - 1, 2 * NA_KW - 1), 0.1),
        'conv_w': nrm(ks[27], (N_ODD, CONV_WIDTH, D_WIDTH), CONV_WIDTH ** -0.5),
        'w_ffn_gate': nrm(ks[28], (DEPTH, D_MODEL, FFN_HIDDEN), D_MODEL ** -0.5),
        'w_ffn_up': nrm(ks[29], (DEPTH, D_MODEL, FFN_HIDDEN), D_MODEL ** -0.5),
        'w_ffn_down': nrm(ks[30], (DEPTH, FFN_HIDDEN, D_MODEL), BETA * FFN_HIDDEN ** -0.5),
    }


def reference(x_prompt, x_sample, cache_k_l0, cache_v_l0, cache_k_l1, cache_v_l1, cache_k_l2, cache_v_l2,
              cache_k_l3, cache_v_l3, c, c_ctx, w_ada, b_ada, ln1_g, ln1_b, ln2_g, ln2_b,
              w_in_ab, w_out_ab, pool_w, pool_scale, q_norm_g, k_norm_g,
              w_in_cd, w_out_cd, na_rpb, conv_w, w_ffn_gate, w_ffn_up, w_ffn_down):
    caches = [(cache_k_l0, cache_v_l0), (cache_k_l1, cache_v_l1), (cache_k_l2, cache_v_l2), (cache_k_l3, cache_v_l3)]
    rope = axial_rope_tables(x_sample.shape[1])
    xp, xs = x_prompt, x_sample
    new_state = []
    for i in range(DEPTH):
        j = i // 2
        sh1_p, sc1_p, g1_p, sh2_p, sc2_p, g2_p = adaln(c_ctx[None, :], w_ada[i], b_ada[i])
        sh1_s, sc1_s, g1_s, sh2_s, sc2_s, g2_s = adaln(c, w_ada[i], b_ada[i])
        hp = modulate(xp, sh1_p, sc1_p)
        hs = modulate(xs, sh1_s, sc1_s)
        ck, cv = caches[i]
        if i % 2 == 0:
            op, kp, vp = mixer_ab(hp, w_in_ab[j], w_out_ab[j], pool_w[j], pool_scale[j], q_norm_g[j], k_norm_g[j])
            os_, _, _ = mixer_ab(hs, w_in_ab[j], w_out_ab[j], pool_w[j], pool_scale[j], q_norm_g[j], k_norm_g[j],
                                 rope=rope, ctx_k=ck, ctx_v=cv)
        else:
            op, kp, vp = mixer_cd(hp, w_in_cd[j], w_out_cd[j], na_rpb[j], conv_w[j])
            os_, _, _ = mixer_cd(hs, w_in_cd[j], w_out_cd[j], na_rpb[j], conv_w[j], ctx_k=ck, ctx_v=cv)
        new_state.append(kp)
        new_state.append(vp)
        xp = layer_norm(ALPHA * xp + g1_p * op, ln1_g[i], ln1_b[i])
        xs = layer_norm(ALPHA * xs + g1_s * os_, ln1_g[i], ln1_b[i])
        fp = swiglu(modulate(xp, sh2_p, sc2_p), w_ffn_gate[i], w_ffn_up[i], w_ffn_down[i])
        fs = swiglu(modulate(xs, sh2_s, sc2_s), w_ffn_gate[i], w_ffn_up[i], w_ffn_down[i])
        xp = layer_norm(ALPHA * xp + g2_p * fp, ln2_g[i], ln2_b[i])
        xs = layer_norm(ALPHA * xs + g2_s * fs, ln2_g[i], ln2_b[i])
    return (xp, xs, *new_state)
```

```python
import functools

import jax
import jax.numpy as jnp
from jax import lax
from jax.experimental import pallas as pl
from jax.experimental.pallas import tpu as pltpu

F32 = jnp.float32
BF16 = jnp.bfloat16

GRID_W = 64
HEAD_DIM = 128
POOL_WINDOWS = (2, 4, 8, 16)
NA_KH = 8
NA_KW = 16
ROPE_THETA = 10000.0
LN_EPS = 1e-5
RMS_EPS = 1e-6
ATTN_SCALE = HEAD_DIM ** -0.5

NA_QROWS = 4
NA_UROWS = NA_KH + NA_QROWS
MASK_VALUE = -1e30

N_COND_ROWS = 8
VMEM_LIMIT = 48 * 1024 * 1024


def _cparams(sem):
    return pltpu.CompilerParams(dimension_semantics=sem, vmem_limit_bytes=VMEM_LIMIT)


def _layer_norm(z, g, b):
    mu = jnp.mean(z, axis=-1, keepdims=True)
    zc = z - mu
    var = jnp.mean(zc * zc, axis=-1, keepdims=True)
    return zc * lax.rsqrt(var + LN_EPS) * g + b


def _dot_nt(a, b):
    return lax.dot_general(a, b, (((1,), (1,)), ((), ())), preferred_element_type=F32)


def _ada_kernel(cond_ref, w_ref, b_ref, o_ref):
    c = cond_ref[...]
    s = (c * jax.nn.sigmoid(c)).astype(BF16)
    o_ref[...] = jnp.dot(s, w_ref[...].astype(BF16), preferred_element_type=F32) + b_ref[...]


def _ada_call(cond, w_ada, b_ada, tn=1024):
    depth, d, n = w_ada.shape
    return pl.pallas_call(
        _ada_kernel,
        grid=(depth, n // tn),
        in_specs=[
            pl.BlockSpec((N_COND_ROWS, d), lambda l, j: (0, 0)),
            pl.BlockSpec((None, d, tn), lambda l, j: (l, 0, j)),
            pl.BlockSpec((None, 1, tn), lambda l, j: (l, 0, j)),
        ],
        out_specs=pl.BlockSpec((None, N_COND_ROWS, tn), lambda l, j: (l, 0, j)),
        out_shape=jax.ShapeDtypeStruct((depth, N_COND_ROWS, n), F32),
        compiler_params=_cparams(("arbitrary", "arbitrary")),
        name="ada",
    )(cond, w_ada, b_ada.reshape(depth, 1, n))


class _Stream:
    def __init__(self, rows, seq_len, first_cond_row, per_seq_cond):
        self.rows = rows
        self.seq_len = seq_len
        self.first_cond_row = first_cond_row
        self.per_seq_cond = per_seq_cond

    def cond_row(self, i, tm):
        if not self.per_seq_cond:
            return self.first_cond_row
        return self.first_cond_row + (i * tm) // self.seq_len


def _mod_spec(stream, layer, chunk, tm, d, grid_rank):
    if grid_rank == 1:
        return pl.BlockSpec((None, None, 1, d), lambda i: (layer, stream.cond_row(i, tm), 0, chunk))
    return pl.BlockSpec((None, None, 1, d), lambda i, j: (layer, stream.cond_row(i, tm), 0, chunk))


def _rms_heads(acc, gain, n_heads):
    outs = []
    for hh in range(n_heads):
        a = acc[:, hh * HEAD_DIM:(hh + 1) * HEAD_DIM]
        ms = jnp.mean(a * a, axis=-1, keepdims=True)
        outs.append(a * lax.rsqrt(ms + RMS_EPS) * gain)
    return outs


def _rope(x, cos, sin_signed):
    return x * cos + pltpu.roll(x, HEAD_DIM // 2, 1) * sin_signed


def _inproj_ab_kernel(*refs, rope, n_q_tiles):
    if rope:
        x_ref, sh_ref, sc_ref, w_ref, qg_ref, kg_ref, cos_ref, sin_ref, p_ref, h_scr = refs
    else:
        x_ref, sh_ref, sc_ref, w_ref, qg_ref, kg_ref, p_ref, k32_ref, v32_ref, h_scr = refs
    j = pl.program_id(1)
    tn = w_ref.shape[1]
    heads_per_tile = tn // HEAD_DIM

    @pl.when(j == 0)
    def _():
        h_scr[...] = (x_ref[...] * (1.0 + sc_ref[...]) + sh_ref[...]).astype(BF16)

    acc = jnp.dot(h_scr[...], w_ref[...], preferred_element_type=F32)

    @pl.when(j < n_q_tiles)
    def _():
        for hh, qn in enumerate(_rms_heads(acc, qg_ref[...] * ATTN_SCALE, heads_per_tile)):
            if rope:
                qn = _rope(qn, cos_ref[...], sin_ref[...])
            p_ref[:, hh * HEAD_DIM:(hh + 1) * HEAD_DIM] = qn.astype(BF16)

    @pl.when(j == n_q_tiles)
    def _():
        p_ref[...] = acc.astype(BF16)

    @pl.when(j == n_q_tiles + 1)
    def _():
        for hh, kn in enumerate(_rms_heads(acc, kg_ref[...], heads_per_tile)):
            sl = slice(hh * HEAD_DIM, (hh + 1) * HEAD_DIM)
            if rope:
                kn = _rope(kn, cos_ref[...], sin_ref[...])
            else:
                k32_ref[:, sl] = kn
            p_ref[:, sl] = kn.astype(BF16)

    @pl.when(j == n_q_tiles + 2)
    def _():
        if not rope:
            v32_ref[...] = acc
        p_ref[...] = acc.astype(BF16)


def _inproj_ab_call(stream, layer, x, mods, w, q_gain, k_gain, rope_tabs, n_q_heads, n_kv_heads,
                    tm=512, tn=512):
    m, d = x.shape
    n = w.shape[1]
    kv_w = n_kv_heads * HEAD_DIM
    assert tn == kv_w and (n_q_heads * HEAD_DIM) % tn == 0
    n_q_tiles = n_q_heads * HEAD_DIM // tn
    rope = rope_tabs is not None
    in_specs = [
        pl.BlockSpec((tm, d), lambda i, j: (i, 0)),
        _mod_spec(stream, layer, 0, tm, d, 2),
        _mod_spec(stream, layer, 1, tm, d, 2),
        pl.BlockSpec((d, tn), lambda i, j: (0, j)),
        pl.BlockSpec((1, HEAD_DIM), lambda i, j: (0, 0)),
        pl.BlockSpec((1, HEAD_DIM), lambda i, j: (0, 0)),
    ]
    args = [x, mods, mods, w, q_gain, k_gain]
    p_spec = pl.BlockSpec((tm, tn), lambda i, j: (i, j))
    p_shape = jax.ShapeDtypeStruct((m, n), BF16)
    if rope:
        tiles_per_seq = stream.seq_len // tm
        tab_spec = pl.BlockSpec((tm, HEAD_DIM), lambda i, j: (i % tiles_per_seq, 0))
        in_specs += [tab_spec, tab_spec]
        args += list(rope_tabs)
        out_specs, out_shape = p_spec, p_shape
    else:
        kv_spec = pl.BlockSpec((tm, kv_w), lambda i, j: (i, 0))
        kv_shape = jax.ShapeDtypeStruct((m, kv_w), F32)
        out_specs, out_shape = (p_spec, kv_spec, kv_spec), (p_shape, kv_shape, kv_shape)
    return pl.pallas_call(
        functools.partial(_inproj_ab_kernel, rope=rope, n_q_tiles=n_q_tiles),
        grid=(m // tm, n // tn),
        in_specs=in_specs,
        out_specs=out_specs,
        out_shape=out_shape,
        scratch_shapes=[pltpu.VMEM((tm, d), BF16)],
        compiler_params=_cparams(("arbitrary", "arbitrary")),
        name="inproj_ab",
    )(*args)


def _inproj_cd_kernel(*refs, emit_kv, n_head_tiles):
    if emit_kv:
        x_ref, sh_ref, sc_ref, w_ref, p_ref, k32_ref, v32_ref, h_scr = refs
    else:
        x_ref, sh_ref, sc_ref, w_ref, p_ref, h_scr = refs
    j = pl.program_id(1)

    @pl.when(j == 0)
    def _():
        h_scr[...] = (x_ref[...] * (1.0 + sc_ref[...]) + sh_ref[...]).astype(BF16)

    acc = jnp.dot(h_scr[...], w_ref[...], preferred_element_type=F32)

    @pl.when(j < n_head_tiles)
    def _():
        p_ref[...] = (acc * ATTN_SCALE).astype(BF16)

    @pl.when(j >= n_head_tiles)
    def _():
        p_ref[...] = acc.astype(BF16)

    if emit_kv:
        @pl.when((j >= n_head_tiles) & (j < 2 * n_head_tiles))
        def _():
            k32_ref[...] = acc

        @pl.when((j >= 2 * n_head_tiles) & (j < 3 * n_head_tiles))
        def _():
            v32_ref[...] = acc


def _inproj_cd_call(stream, layer, x, mods, w, n_heads, emit_kv, tm=512, tn=512):
    m, d = x.shape
    n = w.shape[1]
    hw = n_heads * HEAD_DIM
    assert hw % tn == 0
    nht = hw // tn
    in_specs = [
        pl.BlockSpec((tm, d), lambda i, j: (i, 0)),
        _mod_spec(stream, layer, 0, tm, d, 2),
        _mod_spec(stream, layer, 1, tm, d, 2),
        pl.BlockSpec((d, tn), lambda i, j: (0, j)),
    ]
    p_spec = pl.BlockSpec((tm, tn), lambda i, j: (i, j))
    p_shape = jax.ShapeDtypeStruct((m, n), BF16)
    if emit_kv:
        k_spec = pl.BlockSpec((tm, tn), lambda i, j: (i, jnp.clip(j - nht, 0, nht - 1)))
        v_spec = pl.BlockSpec((tm, tn), lambda i, j: (i, jnp.clip(j - 2 * nht, 0, nht - 1)))
        kv_shape = jax.ShapeDtypeStruct((m, hw), F32)
        out_specs, out_shape = (p_spec, k_spec, v_spec), (p_shape, kv_shape, kv_shape)
    else:
        out_specs, out_shape = p_spec, p_shape
    return pl.pallas_call(
        functools.partial(_inproj_cd_kernel, emit_kv=emit_kv, n_head_tiles=nht),
        grid=(m // tm, n // tn),
        in_specs=in_specs,
        out_specs=out_specs,
        out_shape=out_shape,
        scratch_shapes=[pltpu.VMEM((tm, d), BF16)],
        compiler_params=_cparams(("arbitrary", "arbitrary")),
        name="inproj_cd",
    )(x, mods, mods, w)


def _pool_kernel(u_ref, pw_ref, ps_ref, o_ref, *, seq_len):
    rows = u_ref.shape[0]
    cw = pw_ref.shape[1]
    t = lax.broadcasted_iota(jnp.int32, (rows, cw), 0) & (seq_len - 1)
    for g, w in enumerate(POOL_WINDOWS):
        sl = slice(g * cw, (g + 1) * cw)
        x = u_ref[:, sl].astype(F32)
        acc = jnp.zeros_like(x)
        for dlt in range(-(w // 2), w - w // 2):
            xs = x if dlt == 0 else pltpu.roll(x, (-dlt) % rows, 0)
            valid = (t + dlt >= 0) & (t + dlt <= seq_len - 1)
            acc = acc + jnp.where(valid, xs, 0.0)
        lo = jnp.clip(t - w // 2, 0, seq_len - 1)
        hi = jnp.clip(t - w // 2 + w - 1, 0, seq_len - 1)
        cnt = (hi - lo + 1).astype(F32)
        pooled = (acc / cnt - x).astype(BF16)
        y = jnp.dot(pooled, pw_ref[g], preferred_element_type=F32) * ps_ref[:, sl]
        o_ref[:, sl] = y.astype(BF16)


def _pool_call(stream, p, u_col_block, pool_w, pool_scale, tr):
    m = p.shape[0]
    g, cw, _ = pool_w.shape
    pw = g * cw
    assert tr % stream.seq_len == 0
    return pl.pallas_call(
        functools.partial(_pool_kernel, seq_len=stream.seq_len),
        grid=(m // tr,),
        in_specs=[
            pl.BlockSpec((tr, pw), lambda i: (i, u_col_block)),
            pl.BlockSpec((g, cw, cw), lambda i: (0, 0, 0)),
            pl.BlockSpec((1, pw), lambda i: (0, 0)),
        ],
        out_specs=pl.BlockSpec((tr, pw), lambda i: (i, 0)),
        out_shape=jax.ShapeDtypeStruct((m, pw), BF16),
        compiler_params=_cparams(("arbitrary",)),
        name="pool",
    )(p, pool_w, pool_scale)


def _conv_kernel(xin_ref, gb_ref, gc_ref, cw_ref, o_ref, *, seq_len):
    rows, width = xin_ref.shape
    t = lax.broadcasted_iota(jnp.int32, (rows, width), 0) & (seq_len - 1)
    u = gc_ref[...].astype(F32) * xin_ref[...].astype(F32)
    prev = jnp.where(t >= 1, pltpu.roll(u, 1, 0), 0.0)
    nxt = jnp.where(t <= seq_len - 2, pltpu.roll(u, rows - 1, 0), 0.0)
    y = cw_ref[0:1, :] * prev + cw_ref[1:2, :] * u + cw_ref[2:3, :] * nxt
    o_ref[...] = (gb_ref[...].astype(F32) * y).astype(BF16)


def _conv_call(stream, p, first_col_block, conv_w, tr):
    m = p.shape[0]
    kw, width = conv_w.shape
    assert tr % stream.seq_len == 0
    return pl.pallas_call(
        functools.partial(_conv_kernel, seq_len=stream.seq_len),
        grid=(m // tr,),
        in_specs=[
            pl.BlockSpec((tr, width), lambda i: (i, first_col_block)),
            pl.BlockSpec((tr, width), lambda i: (i, first_col_block + 1)),
            pl.BlockSpec((tr, width), lambda i: (i, first_col_block + 2)),
            pl.BlockSpec((kw, width), lambda i: (0, 0)),
        ],
        out_specs=pl.BlockSpec((tr, width), lambda i: (i, 0)),
        out_shape=jax.ShapeDtypeStruct((m, width), BF16),
        compiler_params=_cparams(("arbitrary",)),
        name="conv",
    )(p, p, p, conv_w)


def _attn_seq_kernel(q_ref, k_ref, v_ref, o_ref, *, n_q_heads, group):
    for h in range(n_q_heads):
        kv = h // group
        q = q_ref[:, h * HEAD_DIM:(h + 1) * HEAD_DIM]
        k = k_ref[:, kv * HEAD_DIM:(kv + 1) * HEAD_DIM]
        v = v_ref[:, kv * HEAD_DIM:(kv + 1) * HEAD_DIM]
        s = _dot_nt(q, k)
        p = jnp.exp(s - jnp.max(s, axis=-1, keepdims=True))
        den = jnp.sum(p, axis=-1, keepdims=True)
        o = jnp.dot(p.astype(BF16), v, preferred_element_type=F32) / den
        o_ref[:, h * HEAD_DIM:(h + 1) * HEAD_DIM] = o.astype(BF16)


def _attn_seq_call(stream, p, n_q_heads, n_kv_heads, q_blk, k_blk, v_blk):
    m = p.shape[0]
    lq = stream.seq_len
    qw, kw = n_q_heads * HEAD_DIM, n_kv_heads * HEAD_DIM
    return pl.pallas_call(
        functools.partial(_attn_seq_kernel, n_q_heads=n_q_heads, group=n_q_heads // n_kv_heads),
        grid=(m // lq,),
        in_specs=[
            pl.BlockSpec((lq, qw), lambda b: (b, q_blk)),
            pl.BlockSpec((lq, kw), lambda b: (b, k_blk)),
            pl.BlockSpec((lq, kw), lambda b: (b, v_blk)),
        ],
        out_specs=pl.BlockSpec((lq, qw), lambda b: (b, 0)),
        out_shape=jax.ShapeDtypeStruct((m, qw), BF16),
        compiler_params=_cparams(("arbitrary",)),
        name="attn_seq",
    )(p, p, p)


def _attn_ctx_kernel(q_ref, k_ref, v_ref, kc_ref, vc_ref, o_ref):
    q = q_ref[...]
    s1 = _dot_nt(q, k_ref[...])
    s2 = _dot_nt(q, kc_ref[...])
    mx = jnp.maximum(jnp.max(s1, axis=-1, keepdims=True), jnp.max(s2, axis=-1, keepdims=True))
    p1 = jnp.exp(s1 - mx)
    p2 = jnp.exp(s2 - mx)
    den = jnp.sum(p1, axis=-1, keepdims=True) + jnp.sum(p2, axis=-1, keepdims=True)
    o = (jnp.dot(p1.astype(BF16), v_ref[...], preferred_element_type=F32)
         + jnp.dot(p2.astype(BF16), vc_ref[...], preferred_element_type=F32))
    o_ref[...] = (o / den).astype(BF16)


def _attn_ctx_call(stream, p, ctx_k, ctx_v, n_q_heads, n_kv_heads, k_col, v_col, tq=512):
    m = p.shape[0]
    ls = stream.seq_len
    nb = m // ls
    nq = ls // tq
    past = ctx_k.shape[0] // nb
    group = n_q_heads // n_kv_heads
    return pl.pallas_call(
        _attn_ctx_kernel,
        grid=(nb, n_q_heads, nq),
        in_specs=[
            pl.BlockSpec((tq, HEAD_DIM), lambda b, h, qi: (b * nq + qi, h)),
            pl.BlockSpec((ls, HEAD_DIM), lambda b, h, qi: (b, k_col + h // group)),
            pl.BlockSpec((ls, HEAD_DIM), lambda b, h, qi: (b, v_col + h // group)),
            pl.BlockSpec((past, HEAD_DIM), lambda b, h, qi: (b, h // group)),
            pl.BlockSpec((past, HEAD_DIM), lambda b, h, qi: (b, h // group)),
        ],
        out_specs=pl.BlockSpec((tq, HEAD_DIM), lambda b, h, qi: (b * nq + qi, h)),
        out_shape=jax.ShapeDtypeStruct((m, n_q_heads * HEAD_DIM), BF16),
        compiler_params=_cparams(("arbitrary", "arbitrary", "arbitrary")),
        name="attn_ctx",
    )(p, p, p, ctx_k, ctx_v)


def _na_bias_kernel(rpb_ref, o_ref, *, n_rows):
    h = pl.program_id(0)
    n_dr = 2 * NA_KH - 1
    n_dc = 2 * NA_KW - 1
    pair_w = 2 * GRID_W
    lane = lax.broadcasted_iota(jnp.int32, (GRID_W, pair_w), 1)
    qcol = lax.broadcasted_iota(jnp.int32, (GRID_W, pair_w), 0)
    kcol = lane & (GRID_W - 1)
    second = lane >= GRID_W
    col_start = jnp.clip(qcol - NA_KW // 2, 0, GRID_W - NA_KW)
    col_ok = (kcol >= col_start) & (kcol < col_start + NA_KW)
    dc_idx = jnp.clip(kcol - qcol, -(NA_KW - 1), NA_KW - 1) + NA_KW - 1

    pair_tiles = []
    for dr in range(-1, n_dr):
        dr_a = min(max(dr, 0), n_dr - 1)
        dr_b = min(max(dr + 1, 0), n_dr - 1)
        tile = jnp.zeros((GRID_W, pair_w), F32)
        for dc in range(n_dc):
            base = h * (n_dr * n_dc) + dc
            val = jnp.where(second, rpb_ref[base + dr_b * n_dc], rpb_ref[base + dr_a * n_dc])
            tile = jnp.where(dc_idx == dc, val, tile)
        pair_tiles.append(jnp.where(col_ok, tile, MASK_VALUE))

    krow = lax.broadcasted_iota(jnp.int32, (GRID_W, NA_UROWS * GRID_W), 1) >> (GRID_W.bit_length() - 1)
    n_blocks = n_rows // NA_QROWS
    for pat, blk in enumerate((0, 1, n_blocks - 1)):
        r0 = blk * NA_QROWS
        us = min(max(r0 - NA_KH // 2, 0), n_rows - NA_UROWS)
        for i in range(NA_QROWS):
            r = r0 + i
            rs = min(max(r - NA_KH // 2, 0), n_rows - NA_KH)
            pieces = []
            for jp in range(NA_UROWS // 2):
                dr = (us + 2 * jp) - r + NA_KH - 1
                pieces.append(pair_tiles[min(max(dr, -1), n_dr - 1) + 1])
            strip = jnp.concatenate(pieces, axis=1)
            in_window = (krow >= rs - us) & (krow < rs - us + NA_KH)
            o_ref[pat, i * GRID_W:(i + 1) * GRID_W, :] = jnp.where(in_window, strip, MASK_VALUE)


def _na_bias_call(rpb, n_rows):
    heads = rpb.shape[0]
    assert NA_UROWS % 2 == 0 and n_rows // NA_QROWS >= 3
    q_tok, k_tok = NA_QROWS * GRID_W, NA_UROWS * GRID_W
    return pl.pallas_call(
        functools.partial(_na_bias_kernel, n_rows=n_rows),
        grid=(heads,),
        in_specs=[pl.BlockSpec(memory_space=pltpu.SMEM)],
        out_specs=pl.BlockSpec((3, None, q_tok, k_tok), lambda h: (0, h, 0, 0)),
        out_shape=jax.ShapeDtypeStruct((3, heads, q_tok, k_tok), F32),
        compiler_params=_cparams(("arbitrary",)),
        name="na_bias",
    )(rpb.reshape(-1))


def _natten_kernel(q_ref, k_ref, v_ref, kc_ref, vc_ref, bias_ref, o_ref, *, n_rows):
    rb = pl.program_id(2)
    us = jnp.clip(rb * NA_QROWS - NA_KH // 2, 0, n_rows - NA_UROWS)
    start = pl.multiple_of(us * GRID_W, GRID_W * NA_QROWS)
    n_keys = NA_UROWS * GRID_W
    q = q_ref[...]
    kb = k_ref[pl.ds(start, n_keys), :]
    vb = v_ref[pl.ds(start, n_keys), :]
    s1 = _dot_nt(q, kb) + bias_ref[...]
    s2 = _dot_nt(q, kc_ref[...])
    mx = jnp.maximum(jnp.max(s1, axis=-1, keepdims=True), jnp.max(s2, axis=-1, keepdims=True))
    p1 = jnp.exp(s1 - mx)
    p2 = jnp.exp(s2 - mx)
    den = jnp.sum(p1, axis=-1, keepdims=True) + jnp.sum(p2, axis=-1, keepdims=True)
    o = (jnp.dot(p1.astype(BF16), vb, preferred_element_type=F32)
         + jnp.dot(p2.astype(BF16), vc_ref[...], preferred_element_type=F32))
    o_ref[...] = (o / den).astype(BF16)


def _natten_call(stream, p, ctx_k, ctx_v, bias, n_heads):
    m = p.shape[0]
    ls = stream.seq_len
    nb = m // ls
    n_rows = ls // GRID_W
    n_blocks = n_rows // NA_QROWS
    tq = NA_QROWS * GRID_W
    past = ctx_k.shape[0] // nb

    def pattern(rb):
        return jnp.where(rb == 0, 0, jnp.where(rb == n_blocks - 1, 2, 1))

    return pl.pallas_call(
        functools.partial(_natten_kernel, n_rows=n_rows),
        grid=(nb, n_heads, n_blocks),
        in_specs=[
            pl.BlockSpec((tq, HEAD_DIM), lambda b, h, rb: (b * n_blocks + rb, h)),
            pl.BlockSpec((ls, HEAD_DIM), lambda b, h, rb: (b, n_heads + h)),
            pl.BlockSpec((ls, HEAD_DIM), lambda b, h, rb: (b, 2 * n_heads + h)),
            pl.BlockSpec((past, HEAD_DIM), lambda b, h, rb: (b, h)),
            pl.BlockSpec((past, HEAD_DIM), lambda b, h, rb: (b, h)),
            pl.BlockSpec((None, None, tq, NA_UROWS * GRID_W), lambda b, h, rb: (pattern(rb), h, 0, 0)),
        ],
        out_specs=pl.BlockSpec((tq, HEAD_DIM), lambda b, h, rb: (b * n_blocks + rb, h)),
        out_shape=jax.ShapeDtypeStruct((m, n_heads * HEAD_DIM), BF16),
        compiler_params=_cparams(("arbitrary", "arbitrary", "arbitrary")),
        name="natten",
    )(p, p, p, ctx_k, ctx_v, bias)


def _outproj_kernel(a_ref, b_ref, wa_ref, wb_ref, x_ref, gate_ref, g_ref, beta_ref, o_ref, *, alpha):
    y = (jnp.dot(a_ref[...], wa_ref[...], preferred_element_type=F32)
         + jnp.dot(b_ref[...], wb_ref[...], preferred_element_type=F32))
    o_ref[...] = _layer_norm(alpha * x_ref[...] + gate_ref[...] * y, g_ref[...], beta_ref[...])


def _outproj_call(stream, layer, a, b, wa, wb, x, mods, ln_g, ln_b, alpha, tm=512):
    m, d = x.shape
    ka, kb = a.shape[1], b.shape[1]
    vec = pl.BlockSpec((1, d), lambda i: (0, 0))
    return pl.pallas_call(
        functools.partial(_outproj_kernel, alpha=alpha),
        grid=(m // tm,),
        in_specs=[
            pl.BlockSpec((tm, ka), lambda i: (i, 0)),
            pl.BlockSpec((tm, kb), lambda i: (i, 0)),
            pl.BlockSpec((ka, d), lambda i: (0, 0)),
            pl.BlockSpec((kb, d), lambda i: (0, 0)),
            pl.BlockSpec((tm, d), lambda i: (i, 0)),
            _mod_spec(stream, layer, 2, tm, d, 1),
            vec, vec,
        ],
        out_specs=pl.BlockSpec((tm, d), lambda i: (i, 0)),
        out_shape=jax.ShapeDtypeStruct((m, d), F32),
        compiler_params=_cparams(("arbitrary",)),
        name="outproj_ln",
    )(a, b, wa, wb, x, mods, ln_g, ln_b)


def _ffn_kernel(x_ref, sh_ref, sc_ref, gate_ref, wg_ref, wu_ref, wd_ref, g_ref, beta_ref, o_ref,
                h_scr, acc_scr, *, alpha):
    j = pl.program_id(1)

    @pl.when(j == 0)
    def _():
        h_scr[...] = (x_ref[...] * (1.0 + sc_ref[...]) + sh_ref[...]).astype(BF16)
        acc_scr[...] = jnp.zeros_like(acc_scr)

    h = h_scr[...]
    gt = jnp.dot(h, wg_ref[...], preferred_element_type=F32)
    up = jnp.dot(h, wu_ref[...], preferred_element_type=F32)
    act = (gt * jax.nn.sigmoid(gt) * up).astype(BF16)
    acc_scr[...] += jnp.dot(act, wd_ref[...], preferred_element_type=F32)

    @pl.when(j == pl.num_programs(1) - 1)
    def _():
        o_ref[...] = _layer_norm(alpha * x_ref[...] + gate_ref[...] * acc_scr[...],
                                 g_ref[...], beta_ref[...])


def _ffn_call(stream, layer, x, mods, wg, wu, wd, ln_g, ln_b, alpha, tm=512, tf=512):
    m, d = x.shape
    f = wg.shape[1]
    vec = pl.BlockSpec((1, d), lambda i, j: (0, 0))
    return pl.pallas_call(
        functools.partial(_ffn_kernel, alpha=alpha),
        grid=(m // tm, f // tf),
        in_specs=[
            pl.BlockSpec((tm, d), lambda i, j: (i, 0)),
            _mod_spec(stream, layer, 3, tm, d, 2),
            _mod_spec(stream, layer, 4, tm, d, 2),
            _mod_spec(stream, layer, 5, tm, d, 2),
            pl.BlockSpec((d, tf), lambda i, j: (0, j)),
            pl.BlockSpec((d, tf), lambda i, j: (0, j)),
            pl.BlockSpec((tf, d), lambda i, j: (j, 0)),
            vec, vec,
        ],
        out_specs=pl.BlockSpec((tm, d), lambda i, j: (i, 0)),
        out_shape=jax.ShapeDtypeStruct((m, d), F32),
        scratch_shapes=[pltpu.VMEM((tm, d), BF16), pltpu.VMEM((tm, d), F32)],
        compiler_params=_cparams(("arbitrary", "arbitrary")),
        name="ffn_ln",
    )(x, mods, mods, mods, wg, wu, wd, ln_g, ln_b)


def _rope_tables(n_tokens):
    t = jnp.arange(n_tokens)
    row = (t // GRID_W).astype(F32)
    col = (t % GRID_W).astype(F32)
    half = HEAD_DIM // 2
    inv = 1.0 / (ROPE_THETA ** (jnp.arange(0, half, 2, dtype=F32) / half))
    ang = jnp.concatenate([row[:, None] * inv, col[:, None] * inv], axis=-1)
    cos, sin = jnp.cos(ang), jnp.sin(ang)
    return jnp.concatenate([cos, cos], axis=-1), jnp.concatenate([-sin, sin], axis=-1)


def kernel(x_prompt, x_sample, cache_k_l0, cache_v_l0, cache_k_l1, cache_v_l1, cache_k_l2, cache_v_l2, cache_k_l3, cache_v_l3, c, c_ctx, w_ada, b_ada, ln1_g, ln1_b, ln2_g, ln2_b, w_in_ab, w_out_ab, pool_w, pool_scale, q_norm_g, k_norm_g, w_in_cd, w_out_cd, na_rpb, conv_w, w_ffn_gate, w_ffn_up, w_ffn_down):
    batch, seq, d = x_prompt.shape
    dec_batch, dec_seq, _ = x_sample.shape
    depth = w_ada.shape[0]
    alpha = (2 * depth) ** 0.25
    pool_width = pool_w.shape[1] * pool_w.shape[2]
    b_kv_heads = cache_k_l0.shape[2]
    c_heads = cache_k_l1.shape[2]
    b_q_heads = (w_in_ab.shape[2] - pool_width) // HEAD_DIM - 2 * b_kv_heads
    conv_width = conv_w.shape[2]
    caches = [(cache_k_l0, cache_v_l0), (cache_k_l1, cache_v_l1),
              (cache_k_l2, cache_v_l2), (cache_k_l3, cache_v_l3)]

    prompt = _Stream(batch * seq, seq, 0, False)
    sample = _Stream(dec_batch * dec_seq, dec_seq, 1, True)
    assert 1 + dec_batch <= N_COND_ROWS

    cond = jnp.concatenate(
        [c_ctx[None, :], c, jnp.zeros((N_COND_ROWS - 1 - dec_batch, d), F32)], axis=0)
    mods = _ada_call(cond, w_ada, b_ada).reshape(depth, N_COND_ROWS, 1, 6 * d)
    rope_tabs = _rope_tables(dec_seq)

    xp = x_prompt.reshape(batch * seq, d)
    xs = x_sample.reshape(dec_batch * dec_seq, d)
    new_state = []
    for i in range(depth):
        jj = i // 2
        ck, cv = caches[i]
        n_ctx_heads = ck.shape[2]
        ck2 = ck.reshape(dec_batch * ck.shape[1], n_ctx_heads * HEAD_DIM).astype(BF16)
        cv2 = cv.reshape(dec_batch * cv.shape[1], n_ctx_heads * HEAD_DIM).astype(BF16)
        if i % 2 == 0:
            qw = b_q_heads * HEAD_DIM
            w_in = w_in_ab[jj]
            w_in = jnp.concatenate(
                [w_in[:, pool_width:pool_width + qw], w_in[:, :pool_width], w_in[:, pool_width + qw:]],
                axis=1).astype(BF16)
            w_out = w_out_ab[jj].astype(BF16)
            wa, wb = w_out[:pool_width], w_out[pool_width:]
            pw = pool_w[jj].astype(BF16)
            ps = pool_scale[jj][None, :]
            qg, kg = q_norm_g[jj][None, :], k_norm_g[jj][None, :]
            kvw = b_kv_heads * HEAD_DIM
            u_blk = qw // pool_width
            k_blk = (qw + pool_width) // kvw
            pp, kp, vp = _inproj_ab_call(prompt, i, xp, mods, w_in, qg, kg, None, b_q_heads, b_kv_heads)
            pool_p = _pool_call(prompt, pp, u_blk, pw, ps, tr=1024)
            attn_p = _attn_seq_call(prompt, pp, b_q_heads, b_kv_heads, 0, k_blk, k_blk + 1)
            psm = _inproj_ab_call(sample, i, xs, mods, w_in, qg, kg, rope_tabs, b_q_heads, b_kv_heads)
            pool_s = _pool_call(sample, psm, u_blk, pw, ps, tr=dec_seq)
            k_col = (qw + pool_width) // HEAD_DIM
            attn_s = _attn_ctx_call(sample, psm, ck2, cv2, b_q_heads, b_kv_heads, k_col, k_col + b_kv_heads)
            mix_p, mix_s = (pool_p, attn_p), (pool_s, attn_s)
            new_state += [kp.reshape(batch, seq, b_kv_heads, HEAD_DIM),
                          vp.reshape(batch, seq, b_kv_heads, HEAD_DIM)]
        else:
            hw = c_heads * HEAD_DIM
            w_in = w_in_cd[jj].astype(BF16)
            w_out = w_out_cd[jj].astype(BF16)
            wa, wb = w_out[:hw], w_out[hw:]
            conv_blk = 3 * hw // conv_width
            pp, kp, vp = _inproj_cd_call(prompt, i, xp, mods, w_in, c_heads, True)
            attn_p = _attn_seq_call(prompt, pp, c_heads, c_heads, 0, 1, 2)
            conv_p = _conv_call(prompt, pp, conv_blk, conv_w[jj], tr=1024)
            psm = _inproj_cd_call(sample, i, xs, mods, w_in, c_heads, False)
            bias = _na_bias_call(na_rpb[jj], dec_seq // GRID_W)
            attn_s = _natten_call(sample, psm, ck2, cv2, bias, c_heads)
            conv_s = _conv_call(sample, psm, conv_blk, conv_w[jj], tr=dec_seq)
            mix_p, mix_s = (attn_p, conv_p), (attn_s, conv_s)
            new_state += [kp.reshape(batch, seq, c_heads, HEAD_DIM),
                          vp.reshape(batch, seq, c_heads, HEAD_DIM)]
        g1, b1 = ln1_g[i][None, :], ln1_b[i][None, :]
        g2, b2 = ln2_g[i][None, :], ln2_b[i][None, :]
        wg, wu, wd = w_ffn_gate[i].astype(BF16), w_ffn_up[i].astype(BF16), w_ffn_down[i].astype(BF16)
        xp = _outproj_call(prompt, i, mix_p[0], mix_p[1], wa, wb, xp, mods, g1, b1, alpha)
        xs = _outproj_call(sample, i, mix_s[0], mix_s[1], wa, wb, xs, mods, g1, b1, alpha)
        xp = _ffn_call(prompt, i, xp, mods, wg, wu, wd, g2, b2, alpha)
        xs = _ffn_call(sample, i, xs, mods, wg, wu, wd, g2, b2, alpha)
    return (xp.reshape(batch, seq, d), xs.reshape(dec_batch, dec_seq, d), *new_state)
```

```python
import functools
import math

import jax
import jax.numpy as jnp
from jax import lax
from jax.experimental import pallas as pl
from jax.experimental.pallas import tpu as pltpu

F32 = jnp.float32
BF16 = jnp.bfloat16

GRID_W = 64
HEAD_DIM = 128
POOL_WINDOWS = (2, 4, 8, 16)
NA_KH = 8
NA_KW = 16
ROPE_THETA = 10000.0
LN_EPS = 1e-5
RMS_EPS = 1e-6
LOG2E = math.log2(math.e)
Q_SCALE = HEAD_DIM ** -0.5 * LOG2E

NA_QROWS = 4
NA_UROWS = NA_KH + NA_QROWS
MASK_VALUE = -1e30

N_COND_ROWS = 8
VMEM_LIMIT = 56 * 1024 * 1024


def _cparams(n_grid_dims):
    return pltpu.CompilerParams(dimension_semantics=("arbitrary",) * n_grid_dims,
                                vmem_limit_bytes=VMEM_LIMIT)


def _layer_norm(z, g, b):
    mu = jnp.mean(z, axis=-1, keepdims=True)
    zc = z - mu
    var = jnp.mean(zc * zc, axis=-1, keepdims=True)
    return zc * lax.rsqrt(var + LN_EPS) * g + b


def _dot(a, b):
    return jnp.dot(a, b, preferred_element_type=F32)


def _dot_nt(a, b):
    return lax.dot_general(a, b, (((1,), (1,)), ((), ())), preferred_element_type=F32)


def _softmax_pv(score_blocks, value_blocks):
    mx = functools.reduce(jnp.maximum, [jnp.max(s, axis=-1, keepdims=True) for s in score_blocks])
    probs = [jnp.exp2(s - mx) for s in score_blocks]
    den = sum(jnp.sum(p, axis=-1, keepdims=True) for p in probs)
    out = sum(_dot(p.astype(BF16), v) for p, v in zip(probs, value_blocks))
    return out / den


def _ada_kernel(cond_ref, w_ref, b_ref, o_ref):
    c = cond_ref[...]
    s = (c * jax.nn.sigmoid(c)).astype(BF16)
    o_ref[...] = _dot(s, w_ref[...].astype(BF16)) + b_ref[...]


def _ada_call(cond, w_ada, b_ada, tn=1024):
    depth, d, n = w_ada.shape
    return pl.pallas_call(
        _ada_kernel,
        grid=(depth, n // tn),
        in_specs=[
            pl.BlockSpec((N_COND_ROWS, d), lambda l, j: (0, 0)),
            pl.BlockSpec((None, d, tn), lambda l, j: (l, 0, j)),
            pl.BlockSpec((None, 1, tn), lambda l, j: (l, 0, j)),
        ],
        out_specs=pl.BlockSpec((None, N_COND_ROWS, tn), lambda l, j: (l, 0, j)),
        out_shape=jax.ShapeDtypeStruct((depth, N_COND_ROWS, n), F32),
        compiler_params=_cparams(2),
        name="ada",
    )(cond, w_ada, b_ada.reshape(depth, 1, n))


class _Stream:
    def __init__(self, rows, seq_len, first_cond_row, per_seq_cond):
        self.rows = rows
        self.seq_len = seq_len
        self.n_seqs = rows // seq_len
        self.first_cond_row = first_cond_row
        self.per_seq_cond = per_seq_cond

    def cond_row(self, i, tm):
        if not self.per_seq_cond:
            return self.first_cond_row
        return self.first_cond_row + (i * tm) // self.seq_len


def _mod_spec(stream, layer, chunk, tm, d, grid_rank):
    if grid_rank == 1:
        return pl.BlockSpec((None, None, 1, d), lambda i: (layer, stream.cond_row(i, tm), 0, chunk))
    return pl.BlockSpec((None, None, 1, d), lambda i, j: (layer, stream.cond_row(i, tm), 0, chunk))


def _layer_spec(arr, layer, grid_rank):
    block = (None,) + arr.shape[1:]
    zeros = (0,) * (arr.ndim - 1)
    if grid_rank == 1:
        return pl.BlockSpec(block, lambda i: (layer,) + zeros)
    if grid_rank == 2:
        return pl.BlockSpec(block, lambda i, j: (layer,) + zeros)
    return pl.BlockSpec(block, lambda i, j, k: (layer,) + zeros)


def _rms_heads(acc, gain, n_heads):
    outs = []
    for hh in range(n_heads):
        a = acc[:, hh * HEAD_DIM:(hh + 1) * HEAD_DIM]
        ms = jnp.mean(a * a, axis=-1, keepdims=True)
        outs.append(a * lax.rsqrt(ms + RMS_EPS) * gain)
    return outs


def _rope(x, cos, sin_signed):
    return x * cos + pltpu.roll(x, HEAD_DIM // 2, 1) * sin_signed


def _store_heads_4d(ref, head0, blocks, seq_len):
    for hh, val in enumerate(blocks):
        for b in range(ref.shape[0]):
            ref[b, :, head0 + hh, :] = val[b * seq_len:(b + 1) * seq_len, :]


def _inproj_ab_kernel(*refs, rope, sub, seq_len):
    if rope:
        (x_ref, sh_ref, sc_ref, w_ref, qg_ref, kg_ref, cos_ref, sin_ref,
         q_ref, u_ref, k_ref, v_ref, h_scr) = refs
    else:
        (x_ref, sh_ref, sc_ref, w_ref, qg_ref, kg_ref,
         q_ref, u_ref, k_ref, v_ref, k32_ref, v32_ref, h_scr) = refs
    j = pl.program_id(1)
    heads = sub // HEAD_DIM

    @pl.when(j == 0)
    def _():
        h_scr[...] = (x_ref[...] * (1.0 + sc_ref[...]) + sh_ref[...]).astype(BF16)

    def sub_dot(s):
        return _dot(h_scr[...], w_ref[:, s * sub:(s + 1) * sub])

    def normed(acc, gain):
        outs = _rms_heads(acc, gain, heads)
        if rope:
            outs = [_rope(o, cos_ref[...], sin_ref[...]) for o in outs]
        return outs

    def write_q(acc, q_tile):
        for hh, qn in enumerate(normed(acc, qg_ref[...] * Q_SCALE)):
            c0 = (q_tile * heads + hh) * HEAD_DIM
            q_ref[:, c0:c0 + HEAD_DIM] = qn.astype(BF16)

    @pl.when(j == 0)
    def _():
        u_ref[...] = sub_dot(0).astype(BF16)
        write_q(sub_dot(1), 0)
        write_q(sub_dot(2), 1)

    @pl.when(j == 1)
    def _():
        write_q(sub_dot(0), 2)
        kn = normed(sub_dot(1), kg_ref[...])
        for hh, val in enumerate(kn):
            k_ref[:, hh * HEAD_DIM:(hh + 1) * HEAD_DIM] = val.astype(BF16)
        vacc = sub_dot(2)
        v_ref[...] = vacc.astype(BF16)
        if not rope:
            _store_heads_4d(k32_ref, 0, kn, seq_len)
            _store_heads_4d(v32_ref, 0, [vacc[:, hh * HEAD_DIM:(hh + 1) * HEAD_DIM] for hh in range(heads)],
                            seq_len)


def _inproj_ab_call(stream, layer, kind_layer, x, mods, w, q_gain, k_gain, rope_tabs, n_kv_heads, tm=512):
    m, d = x.shape
    n = w.shape[2]
    sub = n_kv_heads * HEAD_DIM
    assert n == 6 * sub
    rope = rope_tabs is not None
    assert stream.seq_len % tm == 0 if rope else tm % stream.seq_len == 0
    in_specs = [
        pl.BlockSpec((tm, d), lambda i, j: (i, 0)),
        _mod_spec(stream, layer, 0, tm, d, 2),
        _mod_spec(stream, layer, 1, tm, d, 2),
        pl.BlockSpec((None, d, 3 * sub), lambda i, j: (kind_layer, 0, j)),
        _layer_spec(q_gain, kind_layer, 2),
        _layer_spec(k_gain, kind_layer, 2),
    ]
    args = [x, mods, mods, w, q_gain, k_gain]

    def rows(width):
        return pl.BlockSpec((tm, width), lambda i, j: (i, 0)), jax.ShapeDtypeStruct((m, width), BF16)

    outs = [rows(3 * sub), rows(sub), rows(sub), rows(sub)]
    if rope:
        tiles_per_seq = stream.seq_len // tm
        tab_spec = pl.BlockSpec((tm, HEAD_DIM), lambda i, j: (i % tiles_per_seq, 0))
        in_specs += [tab_spec, tab_spec]
        args += list(rope_tabs)
    else:
        sps = tm // stream.seq_len
        kv_block = (sps, stream.seq_len, n_kv_heads, HEAD_DIM)
        kv = (pl.BlockSpec(kv_block, lambda i, j: (i, 0, 0, 0)),
              jax.ShapeDtypeStruct((stream.n_seqs,) + kv_block[1:], F32))
        outs += [kv, kv]
    return pl.pallas_call(
        functools.partial(_inproj_ab_kernel, rope=rope, sub=sub, seq_len=stream.seq_len),
        grid=(m // tm, 2),
        in_specs=in_specs,
        out_specs=tuple(o[0] for o in outs),
        out_shape=tuple(o[1] for o in outs),
        scratch_shapes=[pltpu.VMEM((tm, d), BF16)],
        compiler_params=_cparams(2),
        name="inproj_ab",
    )(*args)


def _inproj_cd_kernel(*refs, emit_kv, seq_len):
    if emit_kv:
        x_ref, sh_ref, sc_ref, w_ref, q_ref, k_ref, v_ref, c_ref, k32_ref, v32_ref, h_scr = refs
    else:
        x_ref, sh_ref, sc_ref, w_ref, q_ref, k_ref, v_ref, c_ref, h_scr = refs
    j = pl.program_id(1)
    n_sub = 3
    sub = w_ref.shape[1] // n_sub
    heads = sub // HEAD_DIM

    @pl.when(j == 0)
    def _():
        h_scr[...] = (x_ref[...] * (1.0 + sc_ref[...]) + sh_ref[...]).astype(BF16)

    def sub_dot(s):
        return _dot(h_scr[...], w_ref[:, s * sub:(s + 1) * sub])

    def region(out_ref, scale, out32_ref):
        for s in range(n_sub):
            acc = sub_dot(s)
            if scale is not None:
                acc = acc * scale
            out_ref[:, s * sub:(s + 1) * sub] = acc.astype(BF16)
            if out32_ref is not None:
                _store_heads_4d(out32_ref, s * heads,
                                [acc[:, hh * HEAD_DIM:(hh + 1) * HEAD_DIM] for hh in range(heads)], seq_len)

    @pl.when(j == 0)
    def _():
        region(q_ref, Q_SCALE, None)

    @pl.when(j == 1)
    def _():
        region(k_ref, None, k32_ref if emit_kv else None)

    @pl.when(j == 2)
    def _():
        region(v_ref, None, v32_ref if emit_kv else None)

    @pl.when(j == 3)
    def _():
        region(c_ref, None, None)


def _inproj_cd_call(stream, layer, kind_layer, x, mods, w, n_heads, emit_kv, tm=512):
    m, d = x.shape
    n = w.shape[2]
    hw = n_heads * HEAD_DIM
    assert n == 4 * hw and n_heads % 3 == 0
    in_specs = [
        pl.BlockSpec((tm, d), lambda i, j: (i, 0)),
        _mod_spec(stream, layer, 0, tm, d, 2),
        _mod_spec(stream, layer, 1, tm, d, 2),
        pl.BlockSpec((None, d, hw), lambda i, j: (kind_layer, 0, j)),
    ]
    row = (pl.BlockSpec((tm, hw), lambda i, j: (i, 0)), jax.ShapeDtypeStruct((m, hw), BF16))
    outs = [row] * 4
    if emit_kv:
        sps = tm // stream.seq_len
        kv_block = (sps, stream.seq_len, n_heads, HEAD_DIM)
        kv = (pl.BlockSpec(kv_block, lambda i, j: (i, 0, 0, 0)),
              jax.ShapeDtypeStruct((stream.n_seqs,) + kv_block[1:], F32))
        outs += [kv, kv]
    return pl.pallas_call(
        functools.partial(_inproj_cd_kernel, emit_kv=emit_kv, seq_len=stream.seq_len),
        grid=(m // tm, 4),
        in_specs=in_specs,
        out_specs=tuple(o[0] for o in outs),
        out_shape=tuple(o[1] for o in outs),
        scratch_shapes=[pltpu.VMEM((tm, d), BF16)],
        compiler_params=_cparams(2),
        name="inproj_cd",
    )(x, mods, mods, w)


def _pool_kernel(u_ref, pw_ref, ps_ref, o_ref, *, seq_len):
    rows = u_ref.shape[0]
    cw = pw_ref.shape[1]
    t = lax.broadcasted_iota(jnp.int32, (rows, cw), 0) & (seq_len - 1)
    for g, w in enumerate(POOL_WINDOWS):
        sl = slice(g * cw, (g + 1) * cw)
        x = u_ref[:, sl].astype(F32)
        acc = jnp.zeros_like(x)
        for dlt in range(-(w // 2), w - w // 2):
            xs = x if dlt == 0 else pltpu.roll(x, (-dlt) % rows, 0)
            valid = (t + dlt >= 0) & (t + dlt <= seq_len - 1)
            acc = acc + jnp.where(valid, xs, 0.0)
        lo = jnp.clip(t - w // 2, 0, seq_len - 1)
        hi = jnp.clip(t - w // 2 + w - 1, 0, seq_len - 1)
        cnt = (hi - lo + 1).astype(F32)
        pooled = (acc / cnt - x).astype(BF16)
        y = _dot(pooled, pw_ref[g]) * ps_ref[:, sl]
        o_ref[:, sl] = y.astype(BF16)


def _pool_call(stream, kind_layer, u, pool_w, pool_scale, tr):
    m, pw = u.shape
    assert tr % stream.seq_len == 0
    return pl.pallas_call(
        functools.partial(_pool_kernel, seq_len=stream.seq_len),
        grid=(m // tr,),
        in_specs=[
            pl.BlockSpec((tr, pw), lambda i: (i, 0)),
            _layer_spec(pool_w, kind_layer, 1),
            _layer_spec(pool_scale, kind_layer, 1),
        ],
        out_specs=pl.BlockSpec((tr, pw), lambda i: (i, 0)),
        out_shape=jax.ShapeDtypeStruct((m, pw), BF16),
        compiler_params=_cparams(1),
        name="pool",
    )(u, pool_w, pool_scale)


def _conv_kernel(xin_ref, gb_ref, gc_ref, cw_ref, o_ref, *, seq_len):
    rows, width = xin_ref.shape
    t = lax.broadcasted_iota(jnp.int32, (rows, width), 0) & (seq_len - 1)
    u = gc_ref[...].astype(F32) * xin_ref[...].astype(F32)
    prev = jnp.where(t >= 1, pltpu.roll(u, 1, 0), 0.0)
    nxt = jnp.where(t <= seq_len - 2, pltpu.roll(u, rows - 1, 0), 0.0)
    y = cw_ref[0:1, :] * prev + cw_ref[1:2, :] * u + cw_ref[2:3, :] * nxt
    o_ref[...] = (gb_ref[...].astype(F32) * y).astype(BF16)


def _conv_call(stream, kind_layer, cin, conv_w, tr):
    m = cin.shape[0]
    width = conv_w.shape[2]
    assert tr % stream.seq_len == 0 and cin.shape[1] == 3 * width
    return pl.pallas_call(
        functools.partial(_conv_kernel, seq_len=stream.seq_len),
        grid=(m // tr,),
        in_specs=[
            pl.BlockSpec((tr, width), lambda i: (i, 0)),
            pl.BlockSpec((tr, width), lambda i: (i, 1)),
            pl.BlockSpec((tr, width), lambda i: (i, 2)),
            _layer_spec(conv_w, kind_layer, 1),
        ],
        out_specs=pl.BlockSpec((tr, width), lambda i: (i, 0)),
        out_shape=jax.ShapeDtypeStruct((m, width), BF16),
        compiler_params=_cparams(1),
        name="conv",
    )(cin, cin, cin, conv_w)


def _head(ref, h, rows=slice(None)):
    return ref[rows, h * HEAD_DIM:(h + 1) * HEAD_DIM]


def _attn_seq_kernel(q_ref, k_ref, v_ref, o_ref, *, n_q_heads, group, seq_len):
    for sq in range(q_ref.shape[0] // seq_len):
        rows = slice(sq * seq_len, (sq + 1) * seq_len)
        for h in range(n_q_heads):
            kv = h // group
            s = _dot_nt(_head(q_ref, h, rows), _head(k_ref, kv, rows))
            o = _softmax_pv([s], [_head(v_ref, kv, rows)])
            o_ref[rows, h * HEAD_DIM:(h + 1) * HEAD_DIM] = o.astype(BF16)


def _attn_seq_call(stream, q, k, v, seqs_per_step=2):
    m, qw = q.shape
    kw = k.shape[1]
    tr = seqs_per_step * stream.seq_len
    return pl.pallas_call(
        functools.partial(_attn_seq_kernel, n_q_heads=qw // HEAD_DIM, group=qw // kw, seq_len=stream.seq_len),
        grid=(m // tr,),
        in_specs=[
            pl.BlockSpec((tr, qw), lambda b: (b, 0)),
            pl.BlockSpec((tr, kw), lambda b: (b, 0)),
            pl.BlockSpec((tr, kw), lambda b: (b, 0)),
        ],
        out_specs=pl.BlockSpec((tr, qw), lambda b: (b, 0)),
        out_shape=jax.ShapeDtypeStruct((m, qw), BF16),
        compiler_params=_cparams(1),
        name="attn_seq",
    )(q, k, v)


def _attn_ctx_kernel(q_ref, k_ref, v_ref, kc_ref, vc_ref, o_ref):
    for g in range(q_ref.shape[1] // HEAD_DIM):
        q = _head(q_ref, g)
        o = _softmax_pv([_dot_nt(q, k_ref[...]), _dot_nt(q, kc_ref[...])], [v_ref[...], vc_ref[...]])
        o_ref[:, g * HEAD_DIM:(g + 1) * HEAD_DIM] = o.astype(BF16)


def _attn_ctx_call(stream, q, k, v, ctx_k, ctx_v, tq=256):
    m, qw = q.shape
    n_kv_heads = k.shape[1] // HEAD_DIM
    gw = qw // n_kv_heads
    ls = stream.seq_len
    nb = stream.n_seqs
    nq = ls // tq
    past = ctx_k.shape[0] // nb
    return pl.pallas_call(
        _attn_ctx_kernel,
        grid=(nb, n_kv_heads, nq),
        in_specs=[
            pl.BlockSpec((tq, gw), lambda b, h, qi: (b * nq + qi, h)),
            pl.BlockSpec((ls, HEAD_DIM), lambda b, h, qi: (b, h)),
            pl.BlockSpec((ls, HEAD_DIM), lambda b, h, qi: (b, h)),
            pl.BlockSpec((past, HEAD_DIM), lambda b, h, qi: (b, h)),
            pl.BlockSpec((past, HEAD_DIM), lambda b, h, qi: (b, h)),
        ],
        out_specs=pl.BlockSpec((tq, gw), lambda b, h, qi: (b * nq + qi, h)),
        out_shape=jax.ShapeDtypeStruct((m, qw), BF16),
        compiler_params=_cparams(3),
        name="attn_ctx",
    )(q, k, v, ctx_k, ctx_v)


def _na_bias_kernel(rpb_ref, o_ref, *, n_rows):
    h = pl.program_id(0)
    n_dr = 2 * NA_KH - 1
    n_dc = 2 * NA_KW - 1
    pair_w = 2 * GRID_W
    lane = lax.broadcasted_iota(jnp.int32, (GRID_W, pair_w), 1)
    qcol = lax.broadcasted_iota(jnp.int32, (GRID_W, pair_w), 0)
    kcol = lane & (GRID_W - 1)
    second = lane >= GRID_W
    col_start = jnp.clip(qcol - NA_KW // 2, 0, GRID_W - NA_KW)
    col_ok = (kcol >= col_start) & (kcol < col_start + NA_KW)
    dc_idx = jnp.clip(kcol - qcol, -(NA_KW - 1), NA_KW - 1) + NA_KW - 1

    pair_tiles = []
    for dr in range(-1, n_dr):
        dr_a = min(max(dr, 0), n_dr - 1)
        dr_b = min(max(dr + 1, 0), n_dr - 1)
        tile = jnp.zeros((GRID_W, pair_w), F32)
        for dc in range(n_dc):
            base = h * (n_dr * n_dc) + dc
            val = jnp.where(second, rpb_ref[base + dr_b * n_dc], rpb_ref[base + dr_a * n_dc])
            tile = jnp.where(dc_idx == dc, val, tile)
        pair_tiles.append(jnp.where(col_ok, tile * LOG2E, MASK_VALUE))

    krow = lax.broadcasted_iota(jnp.int32, (GRID_W, NA_UROWS * GRID_W), 1) >> (GRID_W.bit_length() - 1)
    n_blocks = n_rows // NA_QROWS
    for pat, blk in enumerate((0, 1, n_blocks - 1)):
        r0 = blk * NA_QROWS
        us = min(max(r0 - NA_KH // 2, 0), n_rows - NA_UROWS)
        for i in range(NA_QROWS):
            r = r0 + i
            rs = min(max(r - NA_KH // 2, 0), n_rows - NA_KH)
            pieces = []
            for jp in range(NA_UROWS // 2):
                dr = (us + 2 * jp) - r + NA_KH - 1
                pieces.append(pair_tiles[min(max(dr, -1), n_dr - 1) + 1])
            strip = jnp.concatenate(pieces, axis=1)
            in_window = (krow >= rs - us) & (krow < rs - us + NA_KH)
            o_ref[pat, i * GRID_W:(i + 1) * GRID_W, :] = jnp.where(in_window, strip, MASK_VALUE)


def _na_bias_call(rpb, n_rows):
    heads = rpb.shape[0]
    assert NA_UROWS % 2 == 0 and n_rows // NA_QROWS >= 3
    q_tok, k_tok = NA_QROWS * GRID_W, NA_UROWS * GRID_W
    return pl.pallas_call(
        functools.partial(_na_bias_kernel, n_rows=n_rows),
        grid=(heads,),
        in_specs=[pl.BlockSpec(memory_space=pltpu.SMEM)],
        out_specs=pl.BlockSpec((3, None, q_tok, k_tok), lambda h: (0, h, 0, 0)),
        out_shape=jax.ShapeDtypeStruct((3, heads, q_tok, k_tok), F32),
        compiler_params=_cparams(1),
        name="na_bias",
    )(rpb.reshape(-1))


def _natten_kernel(q_ref, k_ref, v_ref, kc_ref, vc_ref, bias_ref, o_ref, *, n_rows):
    rb = pl.program_id(2)
    us = jnp.clip(rb * NA_QROWS - NA_KH // 2, 0, n_rows - NA_UROWS)
    start = pl.multiple_of(us * GRID_W, GRID_W * NA_QROWS)
    band = pl.ds(start, NA_UROWS * GRID_W)
    for h in range(q_ref.shape[1] // HEAD_DIM):
        q = _head(q_ref, h)
        s_nb = _dot_nt(q, _head(k_ref, h, band)) + bias_ref[h]
        s_ctx = _dot_nt(q, _head(kc_ref, h))
        o = _softmax_pv([s_nb, s_ctx], [_head(v_ref, h, band), _head(vc_ref, h)])
        o_ref[:, h * HEAD_DIM:(h + 1) * HEAD_DIM] = o.astype(BF16)


def _natten_call(stream, q, k, v, ctx_k, ctx_v, bias, heads_per_step=4):
    m, hw = q.shape
    ls = stream.seq_len
    nb = stream.n_seqs
    n_rows = ls // GRID_W
    n_blocks = n_rows // NA_QROWS
    tq = NA_QROWS * GRID_W
    past = ctx_k.shape[0] // nb
    gw = heads_per_step * HEAD_DIM
    assert hw % gw == 0

    def pattern(rb):
        return jnp.where(rb == 0, 0, jnp.where(rb == n_blocks - 1, 2, 1))

    return pl.pallas_call(
        functools.partial(_natten_kernel, n_rows=n_rows),
        grid=(nb, hw // gw, n_blocks),
        in_specs=[
            pl.BlockSpec((tq, gw), lambda b, h, rb: (b * n_blocks + rb, h)),
            pl.BlockSpec((ls, gw), lambda b, h, rb: (b, h)),
            pl.BlockSpec((ls, gw), lambda b, h, rb: (b, h)),
            pl.BlockSpec((past, gw), lambda b, h, rb: (b, h)),
            pl.BlockSpec((past, gw), lambda b, h, rb: (b, h)),
            pl.BlockSpec((None, heads_per_step, tq, NA_UROWS * GRID_W),
                         lambda b, h, rb: (pattern(rb), h, 0, 0)),
        ],
        out_specs=pl.BlockSpec((tq, gw), lambda b, h, rb: (b * n_blocks + rb, h)),
        out_shape=jax.ShapeDtypeStruct((m, hw), BF16),
        compiler_params=_cparams(3),
        name="natten",
    )(q, k, v, ctx_k, ctx_v, bias)


def _outproj_kernel(a_ref, b_ref, w_ref, x_ref, gate_ref, g_ref, beta_ref, o_ref, *, alpha, n_chunks):
    ka = a_ref.shape[1]
    rc = x_ref.shape[0] // n_chunks
    for c in range(n_chunks):
        rows = slice(c * rc, (c + 1) * rc)
        y = _dot(a_ref[rows, :], w_ref[:ka, :]) + _dot(b_ref[rows, :], w_ref[ka:, :])
        o_ref[rows, :] = _layer_norm(alpha * x_ref[rows, :] + gate_ref[...] * y, g_ref[...], beta_ref[...])


def _outproj_call(stream, layer, kind_layer, a, b, w, x, mods, ln_g, ln_b, alpha, tm=512, n_chunks=2):
    m, d = x.shape
    ka, kb = a.shape[1], b.shape[1]
    return pl.pallas_call(
        functools.partial(_outproj_kernel, alpha=alpha, n_chunks=n_chunks),
        grid=(m // tm,),
        in_specs=[
            pl.BlockSpec((tm, ka), lambda i: (i, 0)),
            pl.BlockSpec((tm, kb), lambda i: (i, 0)),
            _layer_spec(w, kind_layer, 1),
            pl.BlockSpec((tm, d), lambda i: (i, 0)),
            _mod_spec(stream, layer, 2, tm, d, 1),
            _layer_spec(ln_g, layer, 1),
            _layer_spec(ln_b, layer, 1),
        ],
        out_specs=pl.BlockSpec((tm, d), lambda i: (i, 0)),
        out_shape=jax.ShapeDtypeStruct((m, d), F32),
        compiler_params=_cparams(1),
        name="outproj_ln",
    )(a, b, w, x, mods, ln_g, ln_b)


def _ffn_kernel(x_ref, sh_ref, sc_ref, gate_ref, wg_ref, wu_ref, wd_ref, g_ref, beta_ref, o_ref,
                h_scr, acc_scr, *, alpha):
    j = pl.program_id(1)

    @pl.when(j == 0)
    def _():
        h_scr[...] = (x_ref[...] * (1.0 + sc_ref[...]) + sh_ref[...]).astype(BF16)
        acc_scr[...] = jnp.zeros_like(acc_scr)

    h = h_scr[...]
    gt = _dot(h, wg_ref[...])
    up = _dot(h, wu_ref[...])
    act = (gt * jax.nn.sigmoid(gt) * up).astype(BF16)
    acc_scr[...] += _dot(act, wd_ref[...])

    @pl.when(j == pl.num_programs(1) - 1)
    def _():
        o_ref[...] = _layer_norm(alpha * x_ref[...] + gate_ref[...] * acc_scr[...],
                                 g_ref[...], beta_ref[...])


def _ffn_call(stream, layer, x, mods, wg, wu, wd, ln_g, ln_b, alpha, tm=512, tf=512):
    m, d = x.shape
    f = wg.shape[2]
    return pl.pallas_call(
        functools.partial(_ffn_kernel, alpha=alpha),
        grid=(m // tm, f // tf),
        in_specs=[
            pl.BlockSpec((tm, d), lambda i, j: (i, 0)),
            _mod_spec(stream, layer, 3, tm, d, 2),
            _mod_spec(stream, layer, 4, tm, d, 2),
            _mod_spec(stream, layer, 5, tm, d, 2),
            pl.BlockSpec((None, d, tf), lambda i, j: (layer, 0, j)),
            pl.BlockSpec((None, d, tf), lambda i, j: (layer, 0, j)),
            pl.BlockSpec((None, tf, d), lambda i, j: (layer, j, 0)),
            _layer_spec(ln_g, layer, 2),
            _layer_spec(ln_b, layer, 2),
        ],
        out_specs=pl.BlockSpec((tm, d), lambda i, j: (i, 0)),
        out_shape=jax.ShapeDtypeStruct((m, d), F32),
        scratch_shapes=[pltpu.VMEM((tm, d), BF16), pltpu.VMEM((tm, d), F32)],
        compiler_params=_cparams(2),
        name="ffn_ln",
    )(x, mods, mods, mods, wg, wu, wd, ln_g, ln_b)


def _rope_tables(n_tokens):
    t = jnp.arange(n_tokens)
    row = (t // GRID_W).astype(F32)
    col = (t % GRID_W).astype(F32)
    half = HEAD_DIM // 2
    inv = 1.0 / (ROPE_THETA ** (jnp.arange(0, half, 2, dtype=F32) / half))
    ang = jnp.concatenate([row[:, None] * inv, col[:, None] * inv], axis=-1)
    cos, sin = jnp.cos(ang), jnp.sin(ang)
    return jnp.concatenate([cos, cos], axis=-1), jnp.concatenate([-sin, sin], axis=-1)


def kernel(x_prompt, x_sample, cache_k_l0, cache_v_l0, cache_k_l1, cache_v_l1, cache_k_l2, cache_v_l2, cache_k_l3, cache_v_l3, c, c_ctx, w_ada, b_ada, ln1_g, ln1_b, ln2_g, ln2_b, w_in_ab, w_out_ab, pool_w, pool_scale, q_norm_g, k_norm_g, w_in_cd, w_out_cd, na_rpb, conv_w, w_ffn_gate, w_ffn_up, w_ffn_down):
    batch, seq, d = x_prompt.shape
    dec_batch, dec_seq, _ = x_sample.shape
    depth = w_ada.shape[0]
    alpha = (2 * depth) ** 0.25
    b_kv_heads = cache_k_l0.shape[2]
    c_heads = cache_k_l1.shape[2]
    caches = [(cache_k_l0, cache_v_l0), (cache_k_l1, cache_v_l1),
              (cache_k_l2, cache_v_l2), (cache_k_l3, cache_v_l3)]

    prompt = _Stream(batch * seq, seq, 0, False)
    sample = _Stream(dec_batch * dec_seq, dec_seq, 1, True)
    assert 1 + dec_batch <= N_COND_ROWS

    cond = jnp.concatenate(
        [c_ctx[None, :], c, jnp.zeros((N_COND_ROWS - 1 - dec_batch, d), F32)], axis=0)
    mods = _ada_call(cond, w_ada, b_ada).reshape(depth, N_COND_ROWS, 1, 6 * d)
    rope_tabs = _rope_tables(dec_seq)

    w_in_ab, w_out_ab, w_in_cd, w_out_cd, pool_w, w_ffn_gate, w_ffn_up, w_ffn_down = (
        w.astype(BF16) for w in (w_in_ab, w_out_ab, w_in_cd, w_out_cd, pool_w,
                                 w_ffn_gate, w_ffn_up, w_ffn_down))
    ln1_g, ln1_b, ln2_g, ln2_b, pool_scale, q_norm_g, k_norm_g = (
        v[:, None, :] for v in (ln1_g, ln1_b, ln2_g, ln2_b, pool_scale, q_norm_g, k_norm_g))

    xp = x_prompt.reshape(batch * seq, d)
    xs = x_sample.reshape(dec_batch * dec_seq, d)
    new_state = []
    for i in range(depth):
        jj = i // 2
        ck, cv = caches[i]
        ck2 = ck.reshape(dec_batch * ck.shape[1], ck.shape[2] * HEAD_DIM).astype(BF16)
        cv2 = cv.reshape(dec_batch * cv.shape[1], cv.shape[2] * HEAD_DIM).astype(BF16)
        if i % 2 == 0:
            qp, up, kp, vp, k32, v32 = _inproj_ab_call(
                prompt, i, jj, xp, mods, w_in_ab, q_norm_g, k_norm_g, None, b_kv_heads)
            qs, us, ks, vs = _inproj_ab_call(
                sample, i, jj, xs, mods, w_in_ab, q_norm_g, k_norm_g, rope_tabs, b_kv_heads)
            mix_p = (_pool_call(prompt, jj, up, pool_w, pool_scale, tr=1024),
                     _attn_seq_call(prompt, qp, kp, vp))
            mix_s = (_pool_call(sample, jj, us, pool_w, pool_scale, tr=dec_seq),
                     _attn_ctx_call(sample, qs, ks, vs, ck2, cv2))
            w_out = w_out_ab
        else:
            qp, kp, vp, cp, k32, v32 = _inproj_cd_call(prompt, i, jj, xp, mods, w_in_cd, c_heads, True)
            qs, ks, vs, cs = _inproj_cd_call(sample, i, jj, xs, mods, w_in_cd, c_heads, False)
            bias = _na_bias_call(na_rpb[jj], dec_seq // GRID_W)
            mix_p = (_attn_seq_call(prompt, qp, kp, vp),
                     _conv_call(prompt, jj, cp, conv_w, tr=1024))
            mix_s = (_natten_call(sample, qs, ks, vs, ck2, cv2, bias),
                     _conv_call(sample, jj, cs, conv_w, tr=dec_seq))
            w_out = w_out_cd
        new_state += [k32, v32]
        xp = _outproj_call(prompt, i, jj, mix_p[0], mix_p[1], w_out, xp, mods, ln1_g, ln1_b, alpha)
        xs = _outproj_call(sample, i, jj, mix_s[0], mix_s[1], w_out, xs, mods, ln1_g, ln1_b, alpha)
        xp = _ffn_call(prompt, i, xp, mods, w_ffn_gate, w_ffn_up, w_ffn_down, ln2_g, ln2_b, alpha)
        xs = _ffn_call(sample, i, xs, mods, w_ffn_gate, w_ffn_up, w_ffn_down, ln2_g, ln2_b, alpha)
    return (xp.reshape(batch, seq, d), xs.reshape(dec_batch, dec_seq, d), *new_state)
```

```python
import functools
import math

import jax
import jax.numpy as jnp
from jax import lax
from jax.experimental import pallas as pl
from jax.experimental.pallas import tpu as pltpu

F32 = jnp.float32
BF16 = jnp.bfloat16

GRID_W = 64
HEAD_DIM = 128
POOL_WINDOWS = (2, 4, 8, 16)
NA_KH = 8
NA_KW = 16
ROPE_THETA = 10000.0
LN_EPS = 1e-5
RMS_EPS = 1e-6
LOG2E = math.log2(math.e)
Q_SCALE = HEAD_DIM ** -0.5 * LOG2E

NA_QROWS = 4
NA_UROWS = NA_KH + NA_QROWS
MASK_VALUE = -1e30

N_COND_ROWS = 8
N_SUB = 3
VMEM_LIMIT = 56 * 1024 * 1024


def _cparams(n_grid_dims):
    return pltpu.CompilerParams(dimension_semantics=("arbitrary",) * n_grid_dims,
                                vmem_limit_bytes=VMEM_LIMIT)


def _layer_norm(z, g, b):
    mu = jnp.mean(z, axis=-1, keepdims=True)
    zc = z - mu
    var = jnp.mean(zc * zc, axis=-1, keepdims=True)
    return zc * lax.rsqrt(var + LN_EPS) * g + b


def _dot(a, b):
    return jnp.dot(a, b, preferred_element_type=F32)


def _dot_nt(a, b):
    return lax.dot_general(a, b, (((1,), (1,)), ((), ())), preferred_element_type=F32)


def _softmax_pv(score_blocks, value_blocks):
    mx = functools.reduce(jnp.maximum, [jnp.max(s, axis=-1, keepdims=True) for s in score_blocks])
    probs = [jnp.exp2(s - mx) for s in score_blocks]
    den = sum(jnp.sum(p, axis=-1, keepdims=True) for p in probs)
    out = sum(_dot(p.astype(BF16), v) for p, v in zip(probs, value_blocks))
    return out / den


def _ada_kernel(cond_ref, w_ref, b_ref, o_ref):
    c = cond_ref[...]
    s = (c * jax.nn.sigmoid(c)).astype(BF16)
    o_ref[...] = _dot(s, w_ref[...].astype(BF16)) + b_ref[...]


def _ada_call(cond, w_ada, b_ada, tn=1024):
    depth, d, n = w_ada.shape
    return pl.pallas_call(
        _ada_kernel,
        grid=(depth, n // tn),
        in_specs=[
            pl.BlockSpec((N_COND_ROWS, d), lambda l, j: (0, 0)),
            pl.BlockSpec((None, d, tn), lambda l, j: (l, 0, j)),
            pl.BlockSpec((None, 1, tn), lambda l, j: (l, 0, j)),
        ],
        out_specs=pl.BlockSpec((None, N_COND_ROWS, tn), lambda l, j: (l, 0, j)),
        out_shape=jax.ShapeDtypeStruct((depth, N_COND_ROWS, n), F32),
        compiler_params=_cparams(2),
        name="ada",
    )(cond, w_ada, b_ada.reshape(depth, 1, n))


class _Stream:
    def __init__(self, rows, seq_len, first_cond_row, per_seq_cond):
        self.rows = rows
        self.seq_len = seq_len
        self.n_seqs = rows // seq_len
        self.first_cond_row = first_cond_row
        self.per_seq_cond = per_seq_cond

    def cond_row(self, i, tm):
        if not self.per_seq_cond:
            return self.first_cond_row
        return self.first_cond_row + (i * tm) // self.seq_len


def _mod_spec(stream, layer, chunk, tm, d, row_axis=0):
    return pl.BlockSpec((None, None, 1, d),
                        lambda *g: (layer, stream.cond_row(g[row_axis], tm), 0, chunk))


def _layer_spec(arr, layer):
    zeros = (0,) * (arr.ndim - 1)
    return pl.BlockSpec((None,) + arr.shape[1:], lambda *g: (layer,) + zeros)


class _Cols:
    def __init__(self, arr, start, width):
        self.arr, self.start, self.width = arr, start, width

    def block(self, bw):
        assert self.start % bw == 0 and self.width % bw == 0
        return self.start // bw


def _rms_heads(acc, gain, n_heads):
    outs = []
    for hh in range(n_heads):
        a = acc[:, hh * HEAD_DIM:(hh + 1) * HEAD_DIM]
        ms = jnp.mean(a * a, axis=-1, keepdims=True)
        outs.append(a * lax.rsqrt(ms + RMS_EPS) * gain)
    return outs


def _rope(x, cos, sin_signed):
    return x * cos + pltpu.roll(x, HEAD_DIM // 2, 1) * sin_signed


def _split_heads(acc):
    return [acc[:, hh * HEAD_DIM:(hh + 1) * HEAD_DIM] for hh in range(acc.shape[1] // HEAD_DIM)]


def _store_heads_4d(ref, head0, blocks, seq_len):
    for hh, val in enumerate(blocks):
        for b in range(ref.shape[0]):
            ref[b, :, head0 + hh, :] = val[b * seq_len:(b + 1) * seq_len, :]


def _store_heads(ref, head0, blocks):
    for hh, val in enumerate(blocks):
        c0 = (head0 + hh) * HEAD_DIM
        ref[:, c0:c0 + HEAD_DIM] = val.astype(BF16)


def _kv32_index(region, n_tiles):
    return lambda j, i: (jnp.where(j < region, 0, jnp.where(j == region, i, n_tiles - 1)), 0, 0, 0)


def _inproj_ab_kernel(*refs, rope, sub, seq_len):
    if rope:
        x_ref, sh_ref, sc_ref, w_ref, qg_ref, kg_ref, cos_ref, sin_ref, p_ref = refs
    else:
        x_ref, sh_ref, sc_ref, w_ref, qg_ref, kg_ref, p_ref, k32_ref, v32_ref = refs
    j = pl.program_id(0)
    heads = sub // HEAD_DIM

    def modulated():
        return (x_ref[...] * (1.0 + sc_ref[...]) + sh_ref[...]).astype(BF16)

    def sub_dot(h, s):
        return _dot(h, w_ref[:, s * sub:(s + 1) * sub])

    def normed(acc, gain):
        outs = _rms_heads(acc, gain, heads)
        if rope:
            outs = [_rope(o, cos_ref[...], sin_ref[...]) for o in outs]
        return outs

    @pl.when(j == 0)
    def _():
        h = modulated()
        for s in range(N_SUB):
            _store_heads(p_ref, s * heads, normed(sub_dot(h, s), qg_ref[...] * Q_SCALE))

    @pl.when(j == 1)
    def _():
        h = modulated()
        p_ref[:, :sub] = sub_dot(h, 0).astype(BF16)
        kn = normed(sub_dot(h, 1), kg_ref[...])
        _store_heads(p_ref, heads, kn)
        vs = _split_heads(sub_dot(h, 2))
        _store_heads(p_ref, 2 * heads, vs)
        if not rope:
            _store_heads_4d(k32_ref, 0, kn, seq_len)
            _store_heads_4d(v32_ref, 0, vs, seq_len)


def _inproj_ab_call(stream, layer, kind_layer, x, mods, w, q_gain, k_gain, rope_tabs, n_kv_heads, tm=512):
    m, d = x.shape
    n = w.shape[2]
    sub = n_kv_heads * HEAD_DIM
    n_tiles = m // tm
    assert n == 2 * N_SUB * sub
    rope = rope_tabs is not None
    assert stream.seq_len % tm == 0 if rope else tm % stream.seq_len == 0
    in_specs = [
        pl.BlockSpec((tm, d), lambda j, i: (i, 0)),
        _mod_spec(stream, layer, 0, tm, d, row_axis=1),
        _mod_spec(stream, layer, 1, tm, d, row_axis=1),
        pl.BlockSpec((None, d, N_SUB * sub), lambda j, i: (kind_layer, 0, j)),
        _layer_spec(q_gain, kind_layer),
        _layer_spec(k_gain, kind_layer),
    ]
    args = [x, mods, mods, w, q_gain, k_gain]
    out_specs = [pl.BlockSpec((tm, N_SUB * sub), lambda j, i: (i, j))]
    out_shape = [jax.ShapeDtypeStruct((m, n), BF16)]
    if rope:
        tiles_per_seq = stream.seq_len // tm
        tab_spec = pl.BlockSpec((tm, HEAD_DIM), lambda j, i: (i % tiles_per_seq, 0))
        in_specs += [tab_spec, tab_spec]
        args += list(rope_tabs)
    else:
        kv_block = (tm // stream.seq_len, stream.seq_len, n_kv_heads, HEAD_DIM)
        out_specs += [pl.BlockSpec(kv_block, _kv32_index(1, n_tiles))] * 2
        out_shape += [jax.ShapeDtypeStruct((stream.n_seqs,) + kv_block[1:], F32)] * 2
    return pl.pallas_call(
        functools.partial(_inproj_ab_kernel, rope=rope, sub=sub, seq_len=stream.seq_len),
        grid=(2, n_tiles),
        in_specs=in_specs,
        out_specs=tuple(out_specs),
        out_shape=tuple(out_shape),
        compiler_params=_cparams(2),
        name="inproj_ab",
    )(*args)


def _inproj_cd_kernel(*refs, emit_kv, seq_len):
    if emit_kv:
        x_ref, sh_ref, sc_ref, w_ref, p_ref, k32_ref, v32_ref = refs
    else:
        x_ref, sh_ref, sc_ref, w_ref, p_ref = refs
    j = pl.program_id(0)
    sub = w_ref.shape[1] // N_SUB
    heads = sub // HEAD_DIM

    def region(scale, out32_ref):
        h = (x_ref[...] * (1.0 + sc_ref[...]) + sh_ref[...]).astype(BF16)
        for s in range(N_SUB):
            acc = _dot(h, w_ref[:, s * sub:(s + 1) * sub])
            if scale is not None:
                acc = acc * scale
            p_ref[:, s * sub:(s + 1) * sub] = acc.astype(BF16)
            if out32_ref is not None:
                _store_heads_4d(out32_ref, s * heads, _split_heads(acc), seq_len)

    @pl.when(j == 0)
    def _():
        region(Q_SCALE, None)

    @pl.when(j == 1)
    def _():
        region(None, k32_ref if emit_kv else None)

    @pl.when(j == 2)
    def _():
        region(None, v32_ref if emit_kv else None)

    @pl.when(j == 3)
    def _():
        region(None, None)


def _inproj_cd_call(stream, layer, kind_layer, x, mods, w, n_heads, emit_kv, tm=512):
    m, d = x.shape
    n = w.shape[2]
    hw = n_heads * HEAD_DIM
    n_tiles = m // tm
    assert n == 4 * hw and n_heads % N_SUB == 0
    in_specs = [
        pl.BlockSpec((tm, d), lambda j, i: (i, 0)),
        _mod_spec(stream, layer, 0, tm, d, row_axis=1),
        _mod_spec(stream, layer, 1, tm, d, row_axis=1),
        pl.BlockSpec((None, d, hw), lambda j, i: (kind_layer, 0, j)),
    ]
    out_specs = [pl.BlockSpec((tm, hw), lambda j, i: (i, j))]
    out_shape = [jax.ShapeDtypeStruct((m, n), BF16)]
    if emit_kv:
        kv_block = (tm // stream.seq_len, stream.seq_len, n_heads, HEAD_DIM)
        out_specs += [pl.BlockSpec(kv_block, _kv32_index(1, n_tiles)),
                      pl.BlockSpec(kv_block, _kv32_index(2, n_tiles))]
        out_shape += [jax.ShapeDtypeStruct((stream.n_seqs,) + kv_block[1:], F32)] * 2
    return pl.pallas_call(
        functools.partial(_inproj_cd_kernel, emit_kv=emit_kv, seq_len=stream.seq_len),
        grid=(4, n_tiles),
        in_specs=in_specs,
        out_specs=tuple(out_specs),
        out_shape=tuple(out_shape),
        compiler_params=_cparams(2),
        name="inproj_cd",
    )(x, mods, mods, w)


def _pool_kernel(u_ref, pw_ref, ps_ref, o_ref, *, seq_len):
    rows = u_ref.shape[0]
    cw = pw_ref.shape[1]
    t = lax.broadcasted_iota(jnp.int32, (rows, cw), 0) & (seq_len - 1)
    for g, w in enumerate(POOL_WINDOWS):
        sl = slice(g * cw, (g + 1) * cw)
        x = u_ref[:, sl].astype(F32)
        acc = jnp.zeros_like(x)
        for dlt in range(-(w // 2), w - w // 2):
            xs = x if dlt == 0 else pltpu.roll(x, (-dlt) % rows, 0)
            valid = (t + dlt >= 0) & (t + dlt <= seq_len - 1)
            acc = acc + jnp.where(valid, xs, 0.0)
        lo = jnp.clip(t - w // 2, 0, seq_len - 1)
        hi = jnp.clip(t - w // 2 + w - 1, 0, seq_len - 1)
        cnt = (hi - lo + 1).astype(F32)
        pooled = (acc / cnt - x).astype(BF16)
        y = _dot(pooled, pw_ref[g]) * ps_ref[:, sl]
        o_ref[:, sl] = y.astype(BF16)


def _pool_call(stream, kind_layer, u, pool_w, pool_scale, tr):
    m, pw = u.arr.shape[0], u.width
    ub = u.block(pw)
    assert tr % stream.seq_len == 0
    return pl.pallas_call(
        functools.partial(_pool_kernel, seq_len=stream.seq_len),
        grid=(m // tr,),
        in_specs=[
            pl.BlockSpec((tr, pw), lambda i: (i, ub)),
            _layer_spec(pool_w, kind_layer),
            _layer_spec(pool_scale, kind_layer),
        ],
        out_specs=pl.BlockSpec((tr, pw), lambda i: (i, 0)),
        out_shape=jax.ShapeDtypeStruct((m, pw), BF16),
        compiler_params=_cparams(1),
        name="pool",
    )(u.arr, pool_w, pool_scale)


def _conv_kernel(xin_ref, gb_ref, gc_ref, cw_ref, o_ref, *, seq_len):
    rows, width = xin_ref.shape
    t = lax.broadcasted_iota(jnp.int32, (rows, width), 0) & (seq_len - 1)
    u = gc_ref[...].astype(F32) * xin_ref[...].astype(F32)
    prev = jnp.where(t >= 1, pltpu.roll(u, 1, 0), 0.0)
    nxt = jnp.where(t <= seq_len - 2, pltpu.roll(u, rows - 1, 0), 0.0)
    y = cw_ref[0:1, :] * prev + cw_ref[1:2, :] * u + cw_ref[2:3, :] * nxt
    o_ref[...] = (gb_ref[...].astype(F32) * y).astype(BF16)


def _conv_call(stream, kind_layer, cin, conv_w, tr):
    m = cin.arr.shape[0]
    width = conv_w.shape[2]
    cb = cin.block(width)
    assert tr % stream.seq_len == 0 and cin.width == 3 * width
    return pl.pallas_call(
        functools.partial(_conv_kernel, seq_len=stream.seq_len),
        grid=(m // tr,),
        in_specs=[
            pl.BlockSpec((tr, width), lambda i: (i, cb)),
            pl.BlockSpec((tr, width), lambda i: (i, cb + 1)),
            pl.BlockSpec((tr, width), lambda i: (i, cb + 2)),
            _layer_spec(conv_w, kind_layer),
        ],
        out_specs=pl.BlockSpec((tr, width), lambda i: (i, 0)),
        out_shape=jax.ShapeDtypeStruct((m, width), BF16),
        compiler_params=_cparams(1),
        name="conv",
    )(cin.arr, cin.arr, cin.arr, conv_w)


def _head(ref, h, rows=slice(None)):
    return ref[rows, h * HEAD_DIM:(h + 1) * HEAD_DIM]


def _attn_seq_kernel(q_ref, k_ref, v_ref, o_ref, *, n_q_heads, group, seq_len):
    for sq in range(q_ref.shape[0] // seq_len):
        rows = slice(sq * seq_len, (sq + 1) * seq_len)
        for h in range(n_q_heads):
            kv = h // group
            s = _dot_nt(_head(q_ref, h, rows), _head(k_ref, kv, rows))
            o = _softmax_pv([s], [_head(v_ref, kv, rows)])
            o_ref[rows, h * HEAD_DIM:(h + 1) * HEAD_DIM] = o.astype(BF16)


def _attn_seq_call(stream, q, k, v, seqs_per_step=2):
    m = q.arr.shape[0]
    qw, kw = q.width, k.width
    qb, kb, vb = q.block(qw), k.block(kw), v.block(kw)
    tr = seqs_per_step * stream.seq_len
    return pl.pallas_call(
        functools.partial(_attn_seq_kernel, n_q_heads=qw // HEAD_DIM, group=qw // kw, seq_len=stream.seq_len),
        grid=(m // tr,),
        in_specs=[
            pl.BlockSpec((tr, qw), lambda b: (b, qb)),
            pl.BlockSpec((tr, kw), lambda b: (b, kb)),
            pl.BlockSpec((tr, kw), lambda b: (b, vb)),
        ],
        out_specs=pl.BlockSpec((tr, qw), lambda b: (b, 0)),
        out_shape=jax.ShapeDtypeStruct((m, qw), BF16),
        compiler_params=_cparams(1),
        name="attn_seq",
    )(q.arr, k.arr, v.arr)


def _attn_ctx_kernel(q_ref, k_ref, v_ref, kc_ref, vc_ref, o_ref):
    for g in range(q_ref.shape[1] // HEAD_DIM):
        q = _head(q_ref, g)
        o = _softmax_pv([_dot_nt(q, k_ref[...]), _dot_nt(q, kc_ref[...])], [v_ref[...], vc_ref[...]])
        o_ref[:, g * HEAD_DIM:(g + 1) * HEAD_DIM] = o.astype(BF16)


def _attn_ctx_call(stream, q, k, v, ctx_k, ctx_v, tq=256):
    m = q.arr.shape[0]
    qw = q.width
    n_kv_heads = k.width // HEAD_DIM
    gw = qw // n_kv_heads
    qb, kb, vb = q.block(gw), k.block(HEAD_DIM), v.block(HEAD_DIM)
    ls = stream.seq_len
    nb = stream.n_seqs
    nq = ls // tq
    past = ctx_k.shape[0] // nb
    return pl.pallas_call(
        _attn_ctx_kernel,
        grid=(nb, n_kv_heads, nq),
        in_specs=[
            pl.BlockSpec((tq, gw), lambda b, h, qi: (b * nq + qi, qb + h)),
            pl.BlockSpec((ls, HEAD_DIM), lambda b, h, qi: (b, kb + h)),
            pl.BlockSpec((ls, HEAD_DIM), lambda b, h, qi: (b, vb + h)),
            pl.BlockSpec((past, HEAD_DIM), lambda b, h, qi: (b, h)),
            pl.BlockSpec((past, HEAD_DIM), lambda b, h, qi: (b, h)),
        ],
        out_specs=pl.BlockSpec((tq, gw), lambda b, h, qi: (b * nq + qi, h)),
        out_shape=jax.ShapeDtypeStruct((m, qw), BF16),
        compiler_params=_cparams(3),
        name="attn_ctx",
    )(q.arr, k.arr, v.arr, ctx_k, ctx_v)


def _na_bias_kernel(rpb_ref, o_ref, *, n_rows):
    h = pl.program_id(0)
    n_dr = 2 * NA_KH - 1
    n_dc = 2 * NA_KW - 1
    pair_w = 2 * GRID_W
    lane = lax.broadcasted_iota(jnp.int32, (GRID_W, pair_w), 1)
    qcol = lax.broadcasted_iota(jnp.int32, (GRID_W, pair_w), 0)
    kcol = lane & (GRID_W - 1)
    second = lane >= GRID_W
    col_start = jnp.clip(qcol - NA_KW // 2, 0, GRID_W - NA_KW)
    col_ok = (kcol >= col_start) & (kcol < col_start + NA_KW)
    dc_idx = jnp.clip(kcol - qcol, -(NA_KW - 1), NA_KW - 1) + NA_KW - 1

    pair_tiles = []
    for dr in range(-1, n_dr):
        dr_a = min(max(dr, 0), n_dr - 1)
        dr_b = min(max(dr + 1, 0), n_dr - 1)
        tile = jnp.zeros((GRID_W, pair_w), F32)
        for dc in range(n_dc):
            base = h * (n_dr * n_dc) + dc
            val = jnp.where(second, rpb_ref[base + dr_b * n_dc], rpb_ref[base + dr_a * n_dc])
            tile = jnp.where(dc_idx == dc, val, tile)
        pair_tiles.append(jnp.where(col_ok, tile * LOG2E, MASK_VALUE))

    krow = lax.broadcasted_iota(jnp.int32, (GRID_W, NA_UROWS * GRID_W), 1) >> (GRID_W.bit_length() - 1)
    n_blocks = n_rows // NA_QROWS
    for pat, blk in enumerate((0, 1, n_blocks - 1)):
        r0 = blk * NA_QROWS
        us = min(max(r0 - NA_KH // 2, 0), n_rows - NA_UROWS)
        for i in range(NA_QROWS):
            r = r0 + i
            rs = min(max(r - NA_KH // 2, 0), n_rows - NA_KH)
            pieces = []
            for jp in range(NA_UROWS // 2):
                dr = (us + 2 * jp) - r + NA_KH - 1
                pieces.append(pair_tiles[min(max(dr, -1), n_dr - 1) + 1])
            strip = jnp.concatenate(pieces, axis=1)
            in_window = (krow >= rs - us) & (krow < rs - us + NA_KH)
            o_ref[pat, i * GRID_W:(i + 1) * GRID_W, :] = jnp.where(in_window, strip, MASK_VALUE)


def _na_bias_call(rpb, n_rows):
    heads = rpb.shape[0]
    assert NA_UROWS % 2 == 0 and n_rows // NA_QROWS >= 3
    q_tok, k_tok = NA_QROWS * GRID_W, NA_UROWS * GRID_W
    return pl.pallas_call(
        functools.partial(_na_bias_kernel, n_rows=n_rows),
        grid=(heads,),
        in_specs=[pl.BlockSpec(memory_space=pltpu.SMEM)],
        out_specs=pl.BlockSpec((3, None, q_tok, k_tok), lambda h: (0, h, 0, 0)),
        out_shape=jax.ShapeDtypeStruct((3, heads, q_tok, k_tok), F32),
        compiler_params=_cparams(1),
        name="na_bias",
    )(rpb.reshape(-1))


def _natten_kernel(q_ref, k_ref, v_ref, kc_ref, vc_ref, bias_ref, o_ref, *, n_rows):
    rb = pl.program_id(2)
    us = jnp.clip(rb * NA_QROWS - NA_KH // 2, 0, n_rows - NA_UROWS)
    start = pl.multiple_of(us * GRID_W, GRID_W * NA_QROWS)
    band = pl.ds(start, NA_UROWS * GRID_W)
    for h in range(q_ref.shape[1] // HEAD_DIM):
        q = _head(q_ref, h)
        s_nb = _dot_nt(q, _head(k_ref, h, band)) + bias_ref[h]
        s_ctx = _dot_nt(q, _head(kc_ref, h))
        o = _softmax_pv([s_nb, s_ctx], [_head(v_ref, h, band), _head(vc_ref, h)])
        o_ref[:, h * HEAD_DIM:(h + 1) * HEAD_DIM] = o.astype(BF16)


def _natten_call(stream, q, k, v, ctx_k, ctx_v, bias, heads_per_step=4):
    m, hw = q.arr.shape[0], q.width
    ls = stream.seq_len
    nb = stream.n_seqs
    n_rows = ls // GRID_W
    n_blocks = n_rows // NA_QROWS
    tq = NA_QROWS * GRID_W
    past = ctx_k.shape[0] // nb
    gw = heads_per_step * HEAD_DIM
    qb, kb, vb = q.block(gw), k.block(gw), v.block(gw)

    def pattern(rb):
        return jnp.where(rb == 0, 0, jnp.where(rb == n_blocks - 1, 2, 1))

    return pl.pallas_call(
        functools.partial(_natten_kernel, n_rows=n_rows),
        grid=(nb, hw // gw, n_blocks),
        in_specs=[
            pl.BlockSpec((tq, gw), lambda b, h, rb: (b * n_blocks + rb, qb + h)),
            pl.BlockSpec((ls, gw), lambda b, h, rb: (b, kb + h)),
            pl.BlockSpec((ls, gw), lambda b, h, rb: (b, vb + h)),
            pl.BlockSpec((past, gw), lambda b, h, rb: (b, h)),
            pl.BlockSpec((past, gw), lambda b, h, rb: (b, h)),
            pl.BlockSpec((None, heads_per_step, tq, NA_UROWS * GRID_W),
                         lambda b, h, rb: (pattern(rb), h, 0, 0)),
        ],
        out_specs=pl.BlockSpec((tq, gw), lambda b, h, rb: (b * n_blocks + rb, h)),
        out_shape=jax.ShapeDtypeStruct((m, hw), BF16),
        compiler_params=_cparams(3),
        name="natten",
    )(q.arr, k.arr, v.arr, ctx_k, ctx_v, bias)


def _outproj_kernel(a_ref, b_ref, w_ref, x_ref, gate_ref, g_ref, beta_ref, o_ref, *, alpha, n_chunks):
    ka = a_ref.shape[1]
    rc = x_ref.shape[0] // n_chunks
    for c in range(n_chunks):
        rows = slice(c * rc, (c + 1) * rc)
        y = _dot(a_ref[rows, :], w_ref[:ka, :]) + _dot(b_ref[rows, :], w_ref[ka:, :])
        o_ref[rows, :] = _layer_norm(alpha * x_ref[rows, :] + gate_ref[...] * y, g_ref[...], beta_ref[...])


def _outproj_call(stream, layer, kind_layer, a, b, w, x, mods, ln_g, ln_b, alpha, tm=512, n_chunks=2):
    m, d = x.shape
    ka, kb = a.shape[1], b.shape[1]
    return pl.pallas_call(
        functools.partial(_outproj_kernel, alpha=alpha, n_chunks=n_chunks),
        grid=(m // tm,),
        in_specs=[
            pl.BlockSpec((tm, ka), lambda i: (i, 0)),
            pl.BlockSpec((tm, kb), lambda i: (i, 0)),
            _layer_spec(w, kind_layer),
            pl.BlockSpec((tm, d), lambda i: (i, 0)),
            _mod_spec(stream, layer, 2, tm, d),
            _layer_spec(ln_g, layer),
            _layer_spec(ln_b, layer),
        ],
        out_specs=pl.BlockSpec((tm, d), lambda i: (i, 0)),
        out_shape=jax.ShapeDtypeStruct((m, d), F32),
        compiler_params=_cparams(1),
        name="outproj_ln",
    )(a, b, w, x, mods, ln_g, ln_b)


def _ffn_kernel(x_ref, sh_ref, sc_ref, gate_ref, wg_ref, wu_ref, wd_ref, g_ref, beta_ref, o_ref,
                h_scr, acc_scr, *, alpha, n_chunks):
    j = pl.program_id(1)
    last = pl.num_programs(1) - 1

    def hidden_chunk(h):
        gt = _dot(h, wg_ref[...])
        up = _dot(h, wu_ref[...])
        act = (gt * jax.nn.sigmoid(gt) * up).astype(BF16)
        return _dot(act, wd_ref[...])

    @pl.when(j == 0)
    def _():
        h = (x_ref[...] * (1.0 + sc_ref[...]) + sh_ref[...]).astype(BF16)
        h_scr[...] = h
        acc_scr[...] = hidden_chunk(h)

    @pl.when((j > 0) & (j < last))
    def _():
        acc_scr[...] += hidden_chunk(h_scr[...])

    @pl.when(j == last)
    def _():
        rc = x_ref.shape[0] // n_chunks
        for c in range(n_chunks):
            rows = slice(c * rc, (c + 1) * rc)
            y = acc_scr[rows, :] + hidden_chunk(h_scr[rows, :])
            o_ref[rows, :] = _layer_norm(alpha * x_ref[rows, :] + gate_ref[...] * y,
                                         g_ref[...], beta_ref[...])


def _ffn_call(stream, layer, x, mods, wg, wu, wd, ln_g, ln_b, alpha, tm=512, tf=512, n_chunks=2):
    m, d = x.shape
    f = wg.shape[2]
    return pl.pallas_call(
        functools.partial(_ffn_kernel, alpha=alpha, n_chunks=n_chunks),
        grid=(m // tm, f // tf),
        in_specs=[
            pl.BlockSpec((tm, d), lambda i, j: (i, 0)),
            _mod_spec(stream, layer, 3, tm, d),
            _mod_spec(stream, layer, 4, tm, d),
            _mod_spec(stream, layer, 5, tm, d),
            pl.BlockSpec((None, d, tf), lambda i, j: (layer, 0, j)),
            pl.BlockSpec((None, d, tf), lambda i, j: (layer, 0, j)),
            pl.BlockSpec((None, tf, d), lambda i, j: (layer, j, 0)),
            _layer_spec(ln_g, layer),
            _layer_spec(ln_b, layer),
        ],
        out_specs=pl.BlockSpec((tm, d), lambda i, j: (i, 0)),
        out_shape=jax.ShapeDtypeStruct((m, d), F32),
        scratch_shapes=[pltpu.VMEM((tm, d), BF16), pltpu.VMEM((tm, d), F32)],
        compiler_params=_cparams(2),
        name="ffn_ln",
    )(x, mods, mods, mods, wg, wu, wd, ln_g, ln_b)


def _rope_tables(n_tokens):
    t = jnp.arange(n_tokens)
    row = (t // GRID_W).astype(F32)
    col = (t % GRID_W).astype(F32)
    half = HEAD_DIM // 2
    inv = 1.0 / (ROPE_THETA ** (jnp.arange(0, half, 2, dtype=F32) / half))
    ang = jnp.concatenate([row[:, None] * inv, col[:, None] * inv], axis=-1)
    cos, sin = jnp.cos(ang), jnp.sin(ang)
    return jnp.concatenate([cos, cos], axis=-1), jnp.concatenate([-sin, sin], axis=-1)


def kernel(x_prompt, x_sample, cache_k_l0, cache_v_l0, cache_k_l1, cache_v_l1, cache_k_l2, cache_v_l2, cache_k_l3, cache_v_l3, c, c_ctx, w_ada, b_ada, ln1_g, ln1_b, ln2_g, ln2_b, w_in_ab, w_out_ab, pool_w, pool_scale, q_norm_g, k_norm_g, w_in_cd, w_out_cd, na_rpb, conv_w, w_ffn_gate, w_ffn_up, w_ffn_down):
    batch, seq, d = x_prompt.shape
    dec_batch, dec_seq, _ = x_sample.shape
    depth = w_ada.shape[0]
    alpha = (2 * depth) ** 0.25
    b_kv_heads = cache_k_l0.shape[2]
    c_heads = cache_k_l1.shape[2]
    pool_width = pool_w.shape[1] * pool_w.shape[2]
    caches = [(cache_k_l0, cache_v_l0), (cache_k_l1, cache_v_l1),
              (cache_k_l2, cache_v_l2), (cache_k_l3, cache_v_l3)]

    prompt = _Stream(batch * seq, seq, 0, False)
    sample = _Stream(dec_batch * dec_seq, dec_seq, 1, True)
    assert 1 + dec_batch <= N_COND_ROWS

    cond = jnp.concatenate(
        [c_ctx[None, :], c, jnp.zeros((N_COND_ROWS - 1 - dec_batch, d), F32)], axis=0)
    mods = _ada_call(cond, w_ada, b_ada).reshape(depth, N_COND_ROWS, 1, 6 * d)
    rope_tabs = _rope_tables(dec_seq)

    kvw = b_kv_heads * HEAD_DIM
    qw = w_in_ab.shape[2] - pool_width - 2 * kvw
    w_in_ab = jnp.concatenate(
        [w_in_ab[:, :, pool_width:pool_width + qw], w_in_ab[:, :, :pool_width],
         w_in_ab[:, :, pool_width + qw:]], axis=2)
    w_in_ab, w_out_ab, w_in_cd, w_out_cd, pool_w, w_ffn_gate, w_ffn_up, w_ffn_down = (
        w.astype(BF16) for w in (w_in_ab, w_out_ab, w_in_cd, w_out_cd, pool_w,
                                 w_ffn_gate, w_ffn_up, w_ffn_down))
    ln1_g, ln1_b, ln2_g, ln2_b, pool_scale, q_norm_g, k_norm_g = (
        v[:, None, :] for v in (ln1_g, ln1_b, ln2_g, ln2_b, pool_scale, q_norm_g, k_norm_g))

    def split_ab(p):
        return (_Cols(p, 0, qw), _Cols(p, qw, pool_width),
                _Cols(p, qw + pool_width, kvw), _Cols(p, qw + pool_width + kvw, kvw))

    def split_cd(p):
        hw = c_heads * HEAD_DIM
        return _Cols(p, 0, hw), _Cols(p, hw, hw), _Cols(p, 2 * hw, hw), _Cols(p, 3 * hw, hw)

    xp = x_prompt.reshape(batch * seq, d)
    xs = x_sample.reshape(dec_batch * dec_seq, d)
    new_state = []
    for i in range(depth):
        jj = i // 2
        ck, cv = caches[i]
        ck2 = ck.reshape(dec_batch * ck.shape[1], ck.shape[2] * HEAD_DIM).astype(BF16)
        cv2 = cv.reshape(dec_batch * cv.shape[1], cv.shape[2] * HEAD_DIM).astype(BF16)
        if i % 2 == 0:
            pp, k32, v32 = _inproj_ab_call(
                prompt, i, jj, xp, mods, w_in_ab, q_norm_g, k_norm_g, None, b_kv_heads)
            (ps,) = _inproj_ab_call(
                sample, i, jj, xs, mods, w_in_ab, q_norm_g, k_norm_g, rope_tabs, b_kv_heads)
            qp, up, kp, vp = split_ab(pp)
            qs, us, ks, vs = split_ab(ps)
            mix_p = (_pool_call(prompt, jj, up, pool_w, pool_scale, tr=1024),
                     _attn_seq_call(prompt, qp, kp, vp))
            mix_s = (_pool_call(sample, jj, us, pool_w, pool_scale, tr=dec_seq),
                     _attn_ctx_call(sample, qs, ks, vs, ck2, cv2))
            w_out = w_out_ab
        else:
            pp, k32, v32 = _inproj_cd_call(prompt, i, jj, xp, mods, w_in_cd, c_heads, True)
            (ps,) = _inproj_cd_call(sample, i, jj, xs, mods, w_in_cd, c_heads, False)
            qp, kp, vp, cp = split_cd(pp)
            qs, ks, vs, cs = split_cd(ps)
            bias = _na_bias_call(na_rpb[jj], dec_seq // GRID_W)
            mix_p = (_attn_seq_call(prompt, qp, kp, vp),
                     _conv_call(prompt, jj, cp, conv_w, tr=1024))
            mix_s = (_natten_call(sample, qs, ks, vs, ck2, cv2, bias),
                     _conv_call(sample, jj, cs, conv_w, tr=dec_seq))
            w_out = w_out_cd
        new_state += [k32, v32]
        xp = _outproj_call(prompt, i, jj, mix_p[0], mix_p[1], w_out, xp, mods, ln1_g, ln1_b, alpha)
        xs = _outproj_call(sample, i, jj, mix_s[0], mix_s[1], w_out, xs, mods, ln1_g, ln1_b, alpha)
        xp = _ffn_call(prompt, i, xp, mods, w_ffn_gate, w_ffn_up, w_ffn_down, ln2_g, ln2_b, alpha)
        xs = _ffn_call(sample, i, xs, mods, w_ffn_gate, w_ffn_up, w_ffn_down, ln2_g, ln2_b, alpha)
    return (xp.reshape(batch, seq, d), xs.reshape(dec_batch, dec_seq, d), *new_state)
```

```python
import functools
import math

import jax
import jax.numpy as jnp
from jax import lax
from jax.experimental import pallas as pl
from jax.experimental.pallas import tpu as pltpu

F32 = jnp.float32
BF16 = jnp.bfloat16

GRID_W = 64
HEAD_DIM = 128
POOL_WINDOWS = (2, 4, 8, 16)
NA_KH = 8
NA_KW = 16
ROPE_THETA = 10000.0
LN_EPS = 1e-5
RMS_EPS = 1e-6
LOG2E = math.log2(math.e)
Q_SCALE = HEAD_DIM ** -0.5 * LOG2E

NA_QROWS = 4
NA_UROWS = NA_KH + NA_QROWS
MASK_VALUE = -1e30

N_COND_ROWS = 8
N_SUB = 3
VMEM_LIMIT = 56 * 1024 * 1024


def _cparams(n_grid_dims):
    return pltpu.CompilerParams(dimension_semantics=("arbitrary",) * n_grid_dims,
                                vmem_limit_bytes=VMEM_LIMIT)


def _layer_norm(z, g, b):
    mu = jnp.mean(z, axis=-1, keepdims=True)
    zc = z - mu
    var = jnp.mean(zc * zc, axis=-1, keepdims=True)
    return zc * lax.rsqrt(var + LN_EPS) * g + b


def _dot(a, b):
    return jnp.dot(a, b, preferred_element_type=F32)


def _dot_nt(a, b):
    return lax.dot_general(a, b, (((1,), (1,)), ((), ())), preferred_element_type=F32)


def _softmax_pv(score_blocks, value_blocks):
    mx = functools.reduce(jnp.maximum, [jnp.max(s, axis=-1, keepdims=True) for s in score_blocks])
    probs = [jnp.exp2(s - mx) for s in score_blocks]
    den = sum(jnp.sum(p, axis=-1, keepdims=True) for p in probs)
    out = sum(_dot(p.astype(BF16), v) for p, v in zip(probs, value_blocks))
    return out / den


def _ada_kernel(cond_ref, w_ref, b_ref, o_ref):
    c = cond_ref[...]
    s = (c * jax.nn.sigmoid(c)).astype(BF16)
    o_ref[...] = _dot(s, w_ref[...].astype(BF16)) + b_ref[...]


def _ada_call(cond, w_ada, b_ada, tn=1024):
    depth, d, n = w_ada.shape
    return pl.pallas_call(
        _ada_kernel,
        grid=(depth, n // tn),
        in_specs=[
            pl.BlockSpec((N_COND_ROWS, d), lambda l, j: (0, 0)),
            pl.BlockSpec((None, d, tn), lambda l, j: (l, 0, j)),
            pl.BlockSpec((None, 1, tn), lambda l, j: (l, 0, j)),
        ],
        out_specs=pl.BlockSpec((None, N_COND_ROWS, tn), lambda l, j: (l, 0, j)),
        out_shape=jax.ShapeDtypeStruct((depth, N_COND_ROWS, n), F32),
        compiler_params=_cparams(2),
        name="ada",
    )(cond, w_ada, b_ada.reshape(depth, 1, n))


class _Stream:
    def __init__(self, rows, seq_len, first_cond_row, per_seq_cond):
        self.rows = rows
        self.seq_len = seq_len
        self.n_seqs = rows // seq_len
        self.first_cond_row = first_cond_row
        self.per_seq_cond = per_seq_cond

    def cond_row(self, i, tm):
        if not self.per_seq_cond:
            return self.first_cond_row
        return self.first_cond_row + (i * tm) // self.seq_len


def _mod_spec(stream, layer, chunk, tm, d, row_axis=0):
    return pl.BlockSpec((None, None, 1, d),
                        lambda *g: (layer, stream.cond_row(g[row_axis], tm), 0, chunk))


def _layer_spec(arr, layer):
    zeros = (0,) * (arr.ndim - 1)
    return pl.BlockSpec((None,) + arr.shape[1:], lambda *g: (layer,) + zeros)


class _Cols:
    def __init__(self, arr, start, width):
        self.arr, self.start, self.width = arr, start, width

    def block(self, bw):
        assert self.start % bw == 0 and self.width % bw == 0
        return self.start // bw


def _rms_heads(acc, gain, n_heads):
    outs = []
    for hh in range(n_heads):
        a = acc[:, hh * HEAD_DIM:(hh + 1) * HEAD_DIM]
        ms = jnp.mean(a * a, axis=-1, keepdims=True)
        outs.append(a * lax.rsqrt(ms + RMS_EPS) * gain)
    return outs


def _rope(x, cos, sin_signed):
    return x * cos + pltpu.roll(x, HEAD_DIM // 2, 1) * sin_signed


def _split_heads(acc):
    return [acc[:, hh * HEAD_DIM:(hh + 1) * HEAD_DIM] for hh in range(acc.shape[1] // HEAD_DIM)]


def _store_heads_4d(ref, head0, blocks, seq_len):
    for hh, val in enumerate(blocks):
        for b in range(ref.shape[0]):
            ref[b, :, head0 + hh, :] = val[b * seq_len:(b + 1) * seq_len, :]


def _store_heads(ref, head0, blocks):
    for hh, val in enumerate(blocks):
        c0 = (head0 + hh) * HEAD_DIM
        ref[:, c0:c0 + HEAD_DIM] = val.astype(BF16)


def _region_out_index(region, n_tiles, rank):
    return lambda j, i: ((jnp.where(j < region, 0, jnp.where(j == region, i, n_tiles - 1)),)
                         + (0,) * (rank - 1))


def _region_weight_specs(w, kind_layer, sub, blocks_of_region):
    d = w.shape[1]

    def block_of(j, s):
        blk = blocks_of_region[-1][s]
        for r in reversed(range(len(blocks_of_region) - 1)):
            blk = jnp.where(j == r, blocks_of_region[r][s], blk)
        return blk

    return [pl.BlockSpec((None, d, sub), lambda j, i, s=s: (kind_layer, 0, block_of(j, s)),
                         pipeline_mode=pl.Buffered(1)) for s in range(N_SUB)]


def _cast_weights(w_refs, w_scr):
    @pl.when(pl.program_id(1) == 0)
    def _():
        for s, w_ref in enumerate(w_refs):
            w_scr[s] = w_ref[...].astype(BF16)


def _inproj_ab_kernel(*refs, rope, sub, seq_len):
    if rope:
        (x_ref, sh_ref, sc_ref, wa_ref, wb_ref, wc_ref, qg_ref, kg_ref, cos_ref, sin_ref,
         q_ref, p_ref, w_scr) = refs
    else:
        (x_ref, sh_ref, sc_ref, wa_ref, wb_ref, wc_ref, qg_ref, kg_ref,
         q_ref, p_ref, k32_ref, v32_ref, w_scr) = refs
    j = pl.program_id(0)
    heads = sub // HEAD_DIM
    _cast_weights((wa_ref, wb_ref, wc_ref), w_scr)

    def modulated():
        return (x_ref[...] * (1.0 + sc_ref[...]) + sh_ref[...]).astype(BF16)

    def sub_dot(h, s):
        return _dot(h, w_scr[s])

    def normed(acc, gain):
        outs = _rms_heads(acc, gain, heads)
        if rope:
            outs = [_rope(o, cos_ref[...], sin_ref[...]) for o in outs]
        return outs

    @pl.when(j == 0)
    def _():
        h = modulated()
        for s in range(N_SUB):
            _store_heads(q_ref, s * heads, normed(sub_dot(h, s), qg_ref[...] * Q_SCALE))

    @pl.when(j == 1)
    def _():
        h = modulated()
        p_ref[:, :sub] = sub_dot(h, 0).astype(BF16)
        kn = normed(sub_dot(h, 1), kg_ref[...])
        _store_heads(p_ref, heads, kn)
        vs = _split_heads(sub_dot(h, 2))
        _store_heads(p_ref, 2 * heads, vs)
        if not rope:
            _store_heads_4d(k32_ref, 0, kn, seq_len)
            _store_heads_4d(v32_ref, 0, vs, seq_len)


def _inproj_ab_call(stream, layer, kind_layer, x, mods, w, q_gain, k_gain, rope_tabs, n_kv_heads, tm=512):
    m, d = x.shape
    n = w.shape[2]
    sub = n_kv_heads * HEAD_DIM
    n_tiles = m // tm
    assert n == 2 * N_SUB * sub
    rope = rope_tabs is not None
    assert stream.seq_len % tm == 0 if rope else tm % stream.seq_len == 0
    in_specs = [
        pl.BlockSpec((tm, d), lambda j, i: (i, 0)),
        _mod_spec(stream, layer, 0, tm, d, row_axis=1),
        _mod_spec(stream, layer, 1, tm, d, row_axis=1),
        *_region_weight_specs(w, kind_layer, sub, ((1, 2, 3), (0, 4, 5))),
        _layer_spec(q_gain, kind_layer),
        _layer_spec(k_gain, kind_layer),
    ]
    args = [x, mods, mods, w, w, w, q_gain, k_gain]
    out_specs = [pl.BlockSpec((tm, N_SUB * sub), _region_out_index(r, n_tiles, 2)) for r in range(2)]
    out_shape = [jax.ShapeDtypeStruct((m, N_SUB * sub), BF16)] * 2
    if rope:
        tiles_per_seq = stream.seq_len // tm
        tab_spec = pl.BlockSpec((tm, HEAD_DIM), lambda j, i: (i % tiles_per_seq, 0))
        in_specs += [tab_spec, tab_spec]
        args += list(rope_tabs)
    else:
        kv_block = (tm // stream.seq_len, stream.seq_len, n_kv_heads, HEAD_DIM)
        out_specs += [pl.BlockSpec(kv_block, _region_out_index(1, n_tiles, 4))] * 2
        out_shape += [jax.ShapeDtypeStruct((stream.n_seqs,) + kv_block[1:], F32)] * 2
    return pl.pallas_call(
        functools.partial(_inproj_ab_kernel, rope=rope, sub=sub, seq_len=stream.seq_len),
        grid=(2, n_tiles),
        in_specs=in_specs,
        out_specs=tuple(out_specs),
        out_shape=tuple(out_shape),
        scratch_shapes=[pltpu.VMEM((N_SUB, d, sub), BF16)],
        compiler_params=_cparams(2),
        name="inproj_ab",
    )(*args)


def _inproj_cd_kernel(*refs, emit_kv, seq_len):
    if emit_kv:
        (x_ref, sh_ref, sc_ref, wa_ref, wb_ref, wc_ref,
         q_ref, k_ref, v_ref, c_ref, k32_ref, v32_ref, w_scr) = refs
    else:
        x_ref, sh_ref, sc_ref, wa_ref, wb_ref, wc_ref, q_ref, k_ref, v_ref, c_ref, w_scr = refs
        k32_ref = v32_ref = None
    j = pl.program_id(0)
    sub = wa_ref.shape[1]
    heads = sub // HEAD_DIM
    _cast_weights((wa_ref, wb_ref, wc_ref), w_scr)

    def region(out_ref, scale, out32_ref):
        h = (x_ref[...] * (1.0 + sc_ref[...]) + sh_ref[...]).astype(BF16)
        for s in range(N_SUB):
            acc = _dot(h, w_scr[s])
            if scale is not None:
                acc = acc * scale
            out_ref[:, s * sub:(s + 1) * sub] = acc.astype(BF16)
            if out32_ref is not None:
                _store_heads_4d(out32_ref, s * heads, _split_heads(acc), seq_len)

    for r, (out_ref, scale, out32_ref) in enumerate(
            ((q_ref, Q_SCALE, None), (k_ref, None, k32_ref), (v_ref, None, v32_ref), (c_ref, None, None))):
        pl.when(j == r)(functools.partial(region, out_ref, scale, out32_ref))


def _inproj_cd_call(stream, layer, kind_layer, x, mods, w, n_heads, emit_kv, tm=512):
    m, d = x.shape
    n = w.shape[2]
    hw = n_heads * HEAD_DIM
    n_tiles = m // tm
    assert n == 4 * hw and hw % N_SUB == 0
    sub = hw // N_SUB
    n_regions = 4
    in_specs = [
        pl.BlockSpec((tm, d), lambda j, i: (i, 0)),
        _mod_spec(stream, layer, 0, tm, d, row_axis=1),
        _mod_spec(stream, layer, 1, tm, d, row_axis=1),
        *_region_weight_specs(w, kind_layer, sub,
                              tuple(tuple(N_SUB * r + s for s in range(N_SUB)) for r in range(n_regions))),
    ]
    out_specs = [pl.BlockSpec((tm, hw), _region_out_index(r, n_tiles, 2)) for r in range(n_regions)]
    out_shape = [jax.ShapeDtypeStruct((m, hw), BF16)] * n_regions
    if emit_kv:
        kv_block = (tm // stream.seq_len, stream.seq_len, n_heads, HEAD_DIM)
        out_specs += [pl.BlockSpec(kv_block, _region_out_index(1, n_tiles, 4), pipeline_mode=pl.Buffered(1)),
                      pl.BlockSpec(kv_block, _region_out_index(2, n_tiles, 4), pipeline_mode=pl.Buffered(1))]
        out_shape += [jax.ShapeDtypeStruct((stream.n_seqs,) + kv_block[1:], F32)] * 2
    return pl.pallas_call(
        functools.partial(_inproj_cd_kernel, emit_kv=emit_kv, seq_len=stream.seq_len),
        grid=(n_regions, n_tiles),
        in_specs=in_specs,
        out_specs=tuple(out_specs),
        out_shape=tuple(out_shape),
        scratch_shapes=[pltpu.VMEM((N_SUB, d, sub), BF16)],
        compiler_params=_cparams(2),
        name="inproj_cd",
    )(x, mods, mods, w, w, w)


def _pool_kernel(u_ref, pw_ref, ps_ref, o_ref, *, seq_len):
    rows = u_ref.shape[0]
    cw = pw_ref.shape[1]
    t = lax.broadcasted_iota(jnp.int32, (rows, cw), 0) & (seq_len - 1)
    for g, w in enumerate(POOL_WINDOWS):
        sl = slice(g * cw, (g + 1) * cw)
        x = u_ref[:, sl].astype(F32)
        acc = jnp.zeros_like(x)
        for dlt in range(-(w // 2), w - w // 2):
            xs = x if dlt == 0 else pltpu.roll(x, (-dlt) % rows, 0)
            valid = (t + dlt >= 0) & (t + dlt <= seq_len - 1)
            acc = acc + jnp.where(valid, xs, 0.0)
        lo = jnp.clip(t - w // 2, 0, seq_len - 1)
        hi = jnp.clip(t - w // 2 + w - 1, 0, seq_len - 1)
        cnt = (hi - lo + 1).astype(F32)
        pooled = (acc / cnt - x).astype(BF16)
        y = _dot(pooled, pw_ref[g]) * ps_ref[:, sl]
        o_ref[:, sl] = y.astype(BF16)


def _pool_call(stream, kind_layer, u, pool_w, pool_scale, tr):
    m, pw = u.arr.shape[0], u.width
    ub = u.block(pw)
    assert tr % stream.seq_len == 0
    return pl.pallas_call(
        functools.partial(_pool_kernel, seq_len=stream.seq_len),
        grid=(m // tr,),
        in_specs=[
            pl.BlockSpec((tr, pw), lambda i: (i, ub)),
            _layer_spec(pool_w, kind_layer),
            _layer_spec(pool_scale, kind_layer),
        ],
        out_specs=pl.BlockSpec((tr, pw), lambda i: (i, 0)),
        out_shape=jax.ShapeDtypeStruct((m, pw), BF16),
        compiler_params=_cparams(1),
        name="pool",
    )(u.arr, pool_w, pool_scale)


def _conv_kernel(xin_ref, gb_ref, gc_ref, cw_ref, o_ref, *, seq_len):
    rows, width = xin_ref.shape
    t = lax.broadcasted_iota(jnp.int32, (rows, width), 0) & (seq_len - 1)
    u = gc_ref[...].astype(F32) * xin_ref[...].astype(F32)
    prev = jnp.where(t >= 1, pltpu.roll(u, 1, 0), 0.0)
    nxt = jnp.where(t <= seq_len - 2, pltpu.roll(u, rows - 1, 0), 0.0)
    y = cw_ref[0:1, :] * prev + cw_ref[1:2, :] * u + cw_ref[2:3, :] * nxt
    o_ref[...] = (gb_ref[...].astype(F32) * y).astype(BF16)


def _conv_call(stream, kind_layer, cin, conv_w, tr):
    m = cin.arr.shape[0]
    width = conv_w.shape[2]
    cb = cin.block(width)
    assert tr % stream.seq_len == 0 and cin.width == 3 * width
    return pl.pallas_call(
        functools.partial(_conv_kernel, seq_len=stream.seq_len),
        grid=(m // tr,),
        in_specs=[
            pl.BlockSpec((tr, width), lambda i: (i, cb)),
            pl.BlockSpec((tr, width), lambda i: (i, cb + 1)),
            pl.BlockSpec((tr, width), lambda i: (i, cb + 2)),
            _layer_spec(conv_w, kind_layer),
        ],
        out_specs=pl.BlockSpec((tr, width), lambda i: (i, 0)),
        out_shape=jax.ShapeDtypeStruct((m, width), BF16),
        compiler_params=_cparams(1),
        name="conv",
    )(cin.arr, cin.arr, cin.arr, conv_w)


def _head(ref, h, rows=slice(None)):
    return ref[rows, h * HEAD_DIM:(h + 1) * HEAD_DIM]


def _attn_seq_kernel(q_ref, k_ref, v_ref, o_ref, *, n_q_heads, group, seq_len):
    for sq in range(q_ref.shape[0] // seq_len):
        rows = slice(sq * seq_len, (sq + 1) * seq_len)
        for h in range(n_q_heads):
            kv = h // group
            s = _dot_nt(_head(q_ref, h, rows), _head(k_ref, kv, rows))
            o = _softmax_pv([s], [_head(v_ref, kv, rows)])
            o_ref[rows, h * HEAD_DIM:(h + 1) * HEAD_DIM] = o.astype(BF16)


def _attn_seq_call(stream, q, k, v, seqs_per_step=2):
    m = q.arr.shape[0]
    qw, kw = q.width, k.width
    qb, kb, vb = q.block(qw), k.block(kw), v.block(kw)
    tr = seqs_per_step * stream.seq_len
    return pl.pallas_call(
        functools.partial(_attn_seq_kernel, n_q_heads=qw // HEAD_DIM, group=qw // kw, seq_len=stream.seq_len),
        grid=(m // tr,),
        in_specs=[
            pl.BlockSpec((tr, qw), lambda b: (b, qb)),
            pl.BlockSpec((tr, kw), lambda b: (b, kb)),
            pl.BlockSpec((tr, kw), lambda b: (b, vb)),
        ],
        out_specs=pl.BlockSpec((tr, qw), lambda b: (b, 0)),
        out_shape=jax.ShapeDtypeStruct((m, qw), BF16),
        compiler_params=_cparams(1),
        name="attn_seq",
    )(q.arr, k.arr, v.arr)


def _attn_ctx_kernel(q_ref, k_ref, v_ref, kc_ref, vc_ref, o_ref):
    for g in range(q_ref.shape[1] // HEAD_DIM):
        q = _head(q_ref, g)
        o = _softmax_pv([_dot_nt(q, k_ref[...]), _dot_nt(q, kc_ref[...])], [v_ref[...], vc_ref[...]])
        o_ref[:, g * HEAD_DIM:(g + 1) * HEAD_DIM] = o.astype(BF16)


def _attn_ctx_call(stream, q, k, v, ctx_k, ctx_v, tq=256):
    m = q.arr.shape[0]
    qw = q.width
    n_kv_heads = k.width // HEAD_DIM
    gw = qw // n_kv_heads
    qb, kb, vb = q.block(gw), k.block(HEAD_DIM), v.block(HEAD_DIM)
    ls = stream.seq_len
    nb = stream.n_seqs
    nq = ls // tq
    past = ctx_k.shape[0] // nb
    return pl.pallas_call(
        _attn_ctx_kernel,
        grid=(nb, n_kv_heads, nq),
        in_specs=[
            pl.BlockSpec((tq, gw), lambda b, h, qi: (b * nq + qi, qb + h)),
            pl.BlockSpec((ls, HEAD_DIM), lambda b, h, qi: (b, kb + h)),
            pl.BlockSpec((ls, HEAD_DIM), lambda b, h, qi: (b, vb + h)),
            pl.BlockSpec((past, HEAD_DIM), lambda b, h, qi: (b, h)),
            pl.BlockSpec((past, HEAD_DIM), lambda b, h, qi: (b, h)),
        ],
        out_specs=pl.BlockSpec((tq, gw), lambda b, h, qi: (b * nq + qi, h)),
        out_shape=jax.ShapeDtypeStruct((m, qw), BF16),
        compiler_params=_cparams(3),
        name="attn_ctx",
    )(q.arr, k.arr, v.arr, ctx_k, ctx_v)


def _na_bias_kernel(rpb_ref, o_ref, *, n_rows):
    h = pl.program_id(0)
    n_dr = 2 * NA_KH - 1
    n_dc = 2 * NA_KW - 1
    pair_w = 2 * GRID_W
    lane = lax.broadcasted_iota(jnp.int32, (GRID_W, pair_w), 1)
    qcol = lax.broadcasted_iota(jnp.int32, (GRID_W, pair_w), 0)
    kcol = lane & (GRID_W - 1)
    second = lane >= GRID_W
    col_start = jnp.clip(qcol - NA_KW // 2, 0, GRID_W - NA_KW)
    col_ok = (kcol >= col_start) & (kcol < col_start + NA_KW)
    dc_idx = jnp.clip(kcol - qcol, -(NA_KW - 1), NA_KW - 1) + NA_KW - 1

    pair_tiles = []
    for dr in range(-1, n_dr):
        dr_a = min(max(dr, 0), n_dr - 1)
        dr_b = min(max(dr + 1, 0), n_dr - 1)
        tile = jnp.zeros((GRID_W, pair_w), F32)
        for dc in range(n_dc):
            base = h * (n_dr * n_dc) + dc
            val = jnp.where(second, rpb_ref[base + dr_b * n_dc], rpb_ref[base + dr_a * n_dc])
            tile = jnp.where(dc_idx == dc, val, tile)
        pair_tiles.append(jnp.where(col_ok, tile * LOG2E, MASK_VALUE))

    krow = lax.broadcasted_iota(jnp.int32, (GRID_W, NA_UROWS * GRID_W), 1) >> (GRID_W.bit_length() - 1)
    n_blocks = n_rows // NA_QROWS
    for pat, blk in enumerate((0, 1, n_blocks - 1)):
        r0 = blk * NA_QROWS
        us = min(max(r0 - NA_KH // 2, 0), n_rows - NA_UROWS)
        for i in range(NA_QROWS):
            r = r0 + i
            rs = min(max(r - NA_KH // 2, 0), n_rows - NA_KH)
            pieces = []
            for jp in range(NA_UROWS // 2):
                dr = (us + 2 * jp) - r + NA_KH - 1
                pieces.append(pair_tiles[min(max(dr, -1), n_dr - 1) + 1])
            strip = jnp.concatenate(pieces, axis=1)
            in_window = (krow >= rs - us) & (krow < rs - us + NA_KH)
            o_ref[pat, i * GRID_W:(i + 1) * GRID_W, :] = jnp.where(in_window, strip, MASK_VALUE)


def _na_bias_call(rpb, n_rows):
    heads = rpb.shape[0]
    assert NA_UROWS % 2 == 0 and n_rows // NA_QROWS >= 3
    q_tok, k_tok = NA_QROWS * GRID_W, NA_UROWS * GRID_W
    return pl.pallas_call(
        functools.partial(_na_bias_kernel, n_rows=n_rows),
        grid=(heads,),
        in_specs=[pl.BlockSpec(memory_space=pltpu.SMEM)],
        out_specs=pl.BlockSpec((3, None, q_tok, k_tok), lambda h: (0, h, 0, 0)),
        out_shape=jax.ShapeDtypeStruct((3, heads, q_tok, k_tok), F32),
        compiler_params=_cparams(1),
        name="na_bias",
    )(rpb.reshape(-1))


def _natten_kernel(q_ref, k_ref, v_ref, kc_ref, vc_ref, bias_ref, o_ref, *, n_rows):
    rb = pl.program_id(2)
    us = jnp.clip(rb * NA_QROWS - NA_KH // 2, 0, n_rows - NA_UROWS)
    start = pl.multiple_of(us * GRID_W, GRID_W * NA_QROWS)
    band = pl.ds(start, NA_UROWS * GRID_W)
    for h in range(q_ref.shape[1] // HEAD_DIM):
        q = _head(q_ref, h)
        s_nb = _dot_nt(q, _head(k_ref, h, band)) + bias_ref[h]
        s_ctx = _dot_nt(q, _head(kc_ref, h))
        o = _softmax_pv([s_nb, s_ctx], [_head(v_ref, h, band), _head(vc_ref, h)])
        o_ref[:, h * HEAD_DIM:(h + 1) * HEAD_DIM] = o.astype(BF16)


def _natten_call(stream, q, k, v, ctx_k, ctx_v, bias, heads_per_step=4):
    m, hw = q.arr.shape[0], q.width
    ls = stream.seq_len
    nb = stream.n_seqs
    n_rows = ls // GRID_W
    n_blocks = n_rows // NA_QROWS
    tq = NA_QROWS * GRID_W
    past = ctx_k.shape[0] // nb
    gw = heads_per_step * HEAD_DIM
    qb, kb, vb = q.block(gw), k.block(gw), v.block(gw)

    def pattern(rb):
        return jnp.where(rb == 0, 0, jnp.where(rb == n_blocks - 1, 2, 1))

    return pl.pallas_call(
        functools.partial(_natten_kernel, n_rows=n_rows),
        grid=(nb, hw // gw, n_blocks),
        in_specs=[
            pl.BlockSpec((tq, gw), lambda b, h, rb: (b * n_blocks + rb, qb + h)),
            pl.BlockSpec((ls, gw), lambda b, h, rb: (b, kb + h)),
            pl.BlockSpec((ls, gw), lambda b, h, rb: (b, vb + h)),
            pl.BlockSpec((past, gw), lambda b, h, rb: (b, h)),
            pl.BlockSpec((past, gw), lambda b, h, rb: (b, h)),
            pl.BlockSpec((None, heads_per_step, tq, NA_UROWS * GRID_W),
                         lambda b, h, rb: (pattern(rb), h, 0, 0)),
        ],
        out_specs=pl.BlockSpec((tq, gw), lambda b, h, rb: (b * n_blocks + rb, h)),
        out_shape=jax.ShapeDtypeStruct((m, hw), BF16),
        compiler_params=_cparams(3),
        name="natten",
    )(q.arr, k.arr, v.arr, ctx_k, ctx_v, bias)


def _outproj_kernel(a_ref, b_ref, w_ref, x_ref, gate_ref, g_ref, beta_ref, o_ref, w_scr, *, alpha, n_chunks):
    @pl.when(pl.program_id(0) == 0)
    def _():
        w_scr[...] = w_ref[...].astype(BF16)

    ka = a_ref.shape[1]
    rc = x_ref.shape[0] // n_chunks
    for c in range(n_chunks):
        rows = slice(c * rc, (c + 1) * rc)
        y = _dot(a_ref[rows, :], w_scr[:ka, :]) + _dot(b_ref[rows, :], w_scr[ka:, :])
        o_ref[rows, :] = _layer_norm(alpha * x_ref[rows, :] + gate_ref[...] * y, g_ref[...], beta_ref[...])


def _outproj_call(stream, layer, kind_layer, a, b, w, x, mods, ln_g, ln_b, alpha, tm=512, n_chunks=2):
    m, d = x.shape
    ka, kb = a.shape[1], b.shape[1]
    return pl.pallas_call(
        functools.partial(_outproj_kernel, alpha=alpha, n_chunks=n_chunks),
        grid=(m // tm,),
        in_specs=[
            pl.BlockSpec((tm, ka), lambda i: (i, 0)),
            pl.BlockSpec((tm, kb), lambda i: (i, 0)),
            pl.BlockSpec((None,) + w.shape[1:], lambda i: (kind_layer, 0, 0), pipeline_mode=pl.Buffered(1)),
            pl.BlockSpec((tm, d), lambda i: (i, 0)),
            _mod_spec(stream, layer, 2, tm, d),
            _layer_spec(ln_g, layer),
            _layer_spec(ln_b, layer),
        ],
        out_specs=pl.BlockSpec((tm, d), lambda i: (i, 0)),
        out_shape=jax.ShapeDtypeStruct((m, d), F32),
        scratch_shapes=[pltpu.VMEM(w.shape[1:], BF16)],
        compiler_params=_cparams(1),
        name="outproj_ln",
    )(a, b, w, x, mods, ln_g, ln_b)


def _ffn_kernel(x_ref, sh_ref, sc_ref, gate_ref, wg_ref, wu_ref, wd_ref, g_ref, beta_ref, o_ref,
                h_scr, acc_scr, *, alpha, n_chunks):
    j = pl.program_id(1)
    last = pl.num_programs(1) - 1

    def hidden_chunk(h):
        gt = _dot(h, wg_ref[...])
        up = _dot(h, wu_ref[...])
        act = (gt * jax.nn.sigmoid(gt) * up).astype(BF16)
        return _dot(act, wd_ref[...])

    @pl.when(j == 0)
    def _():
        h = (x_ref[...] * (1.0 + sc_ref[...]) + sh_ref[...]).astype(BF16)
        h_scr[...] = h
        acc_scr[...] = hidden_chunk(h)

    @pl.when((j > 0) & (j < last))
    def _():
        acc_scr[...] += hidden_chunk(h_scr[...])

    @pl.when(j == last)
    def _():
        rc = x_ref.shape[0] // n_chunks
        for c in range(n_chunks):
            rows = slice(c * rc, (c + 1) * rc)
            y = acc_scr[rows, :] + hidden_chunk(h_scr[rows, :])
            o_ref[rows, :] = _layer_norm(alpha * x_ref[rows, :] + gate_ref[...] * y,
                                         g_ref[...], beta_ref[...])


def _ffn_call(stream, layer, x, mods, wg, wu, wd, ln_g, ln_b, alpha, tm=512, n_chunks=2):
    m, d = x.shape
    n_blocks, tf = wg.shape[1], wg.shape[3]
    return pl.pallas_call(
        functools.partial(_ffn_kernel, alpha=alpha, n_chunks=n_chunks),
        grid=(m // tm, n_blocks),
        in_specs=[
            pl.BlockSpec((tm, d), lambda i, j: (i, 0)),
            _mod_spec(stream, layer, 3, tm, d),
            _mod_spec(stream, layer, 4, tm, d),
            _mod_spec(stream, layer, 5, tm, d),
            pl.BlockSpec((None, None, d, tf), lambda i, j: (layer, j, 0, 0)),
            pl.BlockSpec((None, None, d, tf), lambda i, j: (layer, j, 0, 0)),
            pl.BlockSpec((None, tf, d), lambda i, j: (layer, j, 0)),
            _layer_spec(ln_g, layer),
            _layer_spec(ln_b, layer),
        ],
        out_specs=pl.BlockSpec((tm, d), lambda i, j: (i, 0)),
        out_shape=jax.ShapeDtypeStruct((m, d), F32),
        scratch_shapes=[pltpu.VMEM((tm, d), BF16), pltpu.VMEM((tm, d), F32)],
        compiler_params=_cparams(2),
        name="ffn_ln",
    )(x, mods, mods, mods, wg, wu, wd, ln_g, ln_b)


def _rope_tables(n_tokens):
    t = jnp.arange(n_tokens)
    row = (t // GRID_W).astype(F32)
    col = (t % GRID_W).astype(F32)
    half = HEAD_DIM // 2
    inv = 1.0 / (ROPE_THETA ** (jnp.arange(0, half, 2, dtype=F32) / half))
    ang = jnp.concatenate([row[:, None] * inv, col[:, None] * inv], axis=-1)
    cos, sin = jnp.cos(ang), jnp.sin(ang)
    return jnp.concatenate([cos, cos], axis=-1), jnp.concatenate([-sin, sin], axis=-1)


def kernel(x_prompt, x_sample, cache_k_l0, cache_v_l0, cache_k_l1, cache_v_l1, cache_k_l2, cache_v_l2, cache_k_l3, cache_v_l3, c, c_ctx, w_ada, b_ada, ln1_g, ln1_b, ln2_g, ln2_b, w_in_ab, w_out_ab, pool_w, pool_scale, q_norm_g, k_norm_g, w_in_cd, w_out_cd, na_rpb, conv_w, w_ffn_gate, w_ffn_up, w_ffn_down):
    batch, seq, d = x_prompt.shape
    dec_batch, dec_seq, _ = x_sample.shape
    depth = w_ada.shape[0]
    alpha = (2 * depth) ** 0.25
    b_kv_heads = cache_k_l0.shape[2]
    c_heads = cache_k_l1.shape[2]
    pool_width = pool_w.shape[1] * pool_w.shape[2]
    caches = [(cache_k_l0, cache_v_l0), (cache_k_l1, cache_v_l1),
              (cache_k_l2, cache_v_l2), (cache_k_l3, cache_v_l3)]

    prompt = _Stream(batch * seq, seq, 0, False)
    sample = _Stream(dec_batch * dec_seq, dec_seq, 1, True)
    assert 1 + dec_batch <= N_COND_ROWS

    cond = jnp.concatenate(
        [c_ctx[None, :], c, jnp.zeros((N_COND_ROWS - 1 - dec_batch, d), F32)], axis=0)
    mods = _ada_call(cond, w_ada, b_ada).reshape(depth, N_COND_ROWS, 1, 6 * d)
    rope_tabs = _rope_tables(dec_seq)

    kvw = b_kv_heads * HEAD_DIM
    assert pool_width == kvw

    def column_blocked(w, tf=512):
        layers, rows, cols = w.shape
        return w.astype(BF16).reshape(layers, rows, cols // tf, tf).transpose(0, 2, 1, 3)

    w_ffn_gate, w_ffn_up = column_blocked(w_ffn_gate), column_blocked(w_ffn_up)
    w_ffn_down, pool_w = w_ffn_down.astype(BF16), pool_w.astype(BF16)
    ln1_g, ln1_b, ln2_g, ln2_b, pool_scale, q_norm_g, k_norm_g = (
        v[:, None, :] for v in (ln1_g, ln1_b, ln2_g, ln2_b, pool_scale, q_norm_g, k_norm_g))

    def whole(arr):
        return _Cols(arr, 0, arr.shape[1])

    def split_ukv(p):
        return _Cols(p, 0, kvw), _Cols(p, kvw, kvw), _Cols(p, 2 * kvw, kvw)

    xp = x_prompt.reshape(batch * seq, d)
    xs = x_sample.reshape(dec_batch * dec_seq, d)
    new_state = []
    for i in range(depth):
        jj = i // 2
        ck, cv = caches[i]
        ck2 = ck.reshape(dec_batch * ck.shape[1], ck.shape[2] * HEAD_DIM).astype(BF16)
        cv2 = cv.reshape(dec_batch * cv.shape[1], cv.shape[2] * HEAD_DIM).astype(BF16)
        if i % 2 == 0:
            qp, ukv_p, k32, v32 = _inproj_ab_call(
                prompt, i, jj, xp, mods, w_in_ab, q_norm_g, k_norm_g, None, b_kv_heads)
            qs, ukv_s = _inproj_ab_call(
                sample, i, jj, xs, mods, w_in_ab, q_norm_g, k_norm_g, rope_tabs, b_kv_heads)
            qp, qs = whole(qp), whole(qs)
            up, kp, vp = split_ukv(ukv_p)
            us, ks, vs = split_ukv(ukv_s)
            mix_p = (_pool_call(prompt, jj, up, pool_w, pool_scale, tr=1024),
                     _attn_seq_call(prompt, qp, kp, vp))
            mix_s = (_pool_call(sample, jj, us, pool_w, pool_scale, tr=dec_seq),
                     _attn_ctx_call(sample, qs, ks, vs, ck2, cv2))
            w_out = w_out_ab
        else:
            qp, kp, vp, cp, k32, v32 = _inproj_cd_call(prompt, i, jj, xp, mods, w_in_cd, c_heads, True)
            qs, ks, vs, cs = _inproj_cd_call(sample, i, jj, xs, mods, w_in_cd, c_heads, False)
            qp, kp, vp, cp, qs, ks, vs, cs = (whole(a) for a in (qp, kp, vp, cp, qs, ks, vs, cs))
            bias = _na_bias_call(na_rpb[jj], dec_seq // GRID_W)
            mix_p = (_attn_seq_call(prompt, qp, kp, vp),
                     _conv_call(prompt, jj, cp, conv_w, tr=1024))
            mix_s = (_natten_call(sample, qs, ks, vs, ck2, cv2, bias),
                     _conv_call(sample, jj, cs, conv_w, tr=dec_seq))
            w_out = w_out_cd
        new_state += [k32, v32]
        xp = _outproj_call(prompt, i, jj, mix_p[0], mix_p[1], w_out, xp, mods, ln1_g, ln1_b, alpha)
        xs = _outproj_call(sample, i, jj, mix_s[0], mix_s[1], w_out, xs, mods, ln1_g, ln1_b, alpha)
        xp = _ffn_call(prompt, i, xp, mods, w_ffn_gate, w_ffn_up, w_ffn_down, ln2_g, ln2_b, alpha)
        xs = _ffn_call(sample, i, xs, mods, w_ffn_gate, w_ffn_up, w_ffn_down, ln2_g, ln2_b, alpha)
    return (xp.reshape(batch, seq, d), xs.reshape(dec_batch, dec_seq, d), *new_state)
```

```python
import functools
import math

import jax
import jax.numpy as jnp
from jax import lax
from jax.experimental import pallas as pl
from jax.experimental.pallas import tpu as pltpu

F32 = jnp.float32
BF16 = jnp.bfloat16

GRID_W = 64
HEAD_DIM = 128
POOL_WINDOWS = (2, 4, 8, 16)
NA_KH = 8
NA_KW = 16
ROPE_THETA = 10000.0
LN_EPS = 1e-5
RMS_EPS = 1e-6
LOG2E = math.log2(math.e)
Q_SCALE = HEAD_DIM ** -0.5 * LOG2E

NA_QROWS = 4
NA_UROWS = NA_KH + NA_QROWS
MASK_VALUE = -1e30

N_COND_ROWS = 8
N_SUB = 3
VMEM_LIMIT = 56 * 1024 * 1024


def _cparams(n_grid_dims):
    return pltpu.CompilerParams(dimension_semantics=("arbitrary",) * n_grid_dims,
                                vmem_limit_bytes=VMEM_LIMIT)


def _layer_norm(z, g, b):
    mu = jnp.mean(z, axis=-1, keepdims=True)
    zc = z - mu
    var = jnp.mean(zc * zc, axis=-1, keepdims=True)
    return zc * lax.rsqrt(var + LN_EPS) * g + b


def _dot(a, b):
    return jnp.dot(a, b, preferred_element_type=F32)


def _dot_nt(a, b):
    return lax.dot_general(a, b, (((1,), (1,)), ((), ())), preferred_element_type=F32)


def _softmax_pv(score_blocks, value_blocks):
    mx = functools.reduce(jnp.maximum, [jnp.max(s, axis=-1, keepdims=True) for s in score_blocks])
    probs = [jnp.exp2(s - mx) for s in score_blocks]
    den = sum(jnp.sum(p, axis=-1, keepdims=True) for p in probs)
    out = sum(_dot(p.astype(BF16), v) for p, v in zip(probs, value_blocks))
    return out / den


def _ada_kernel(cond_ref, w_ref, b_ref, o_ref):
    c = cond_ref[...]
    s = (c * jax.nn.sigmoid(c)).astype(BF16)
    o_ref[...] = _dot(s, w_ref[...].astype(BF16)) + b_ref[...]


def _ada_call(cond, w_ada, b_ada, tn=1024):
    depth, d, n = w_ada.shape
    return pl.pallas_call(
        _ada_kernel,
        grid=(depth, n // tn),
        in_specs=[
            pl.BlockSpec((N_COND_ROWS, d), lambda l, j: (0, 0)),
            pl.BlockSpec((None, d, tn), lambda l, j: (l, 0, j)),
            pl.BlockSpec((None, 1, tn), lambda l, j: (l, 0, j)),
        ],
        out_specs=pl.BlockSpec((None, N_COND_ROWS, tn), lambda l, j: (l, 0, j)),
        out_shape=jax.ShapeDtypeStruct((depth, N_COND_ROWS, n), F32),
        compiler_params=_cparams(2),
        name="ada",
    )(cond, w_ada, b_ada.reshape(depth, 1, n))


class _Stream:
    def __init__(self, rows, seq_len, first_cond_row, per_seq_cond):
        self.rows = rows
        self.seq_len = seq_len
        self.n_seqs = rows // seq_len
        self.first_cond_row = first_cond_row
        self.per_seq_cond = per_seq_cond

    def cond_row(self, i, tm):
        if not self.per_seq_cond:
            return self.first_cond_row
        return self.first_cond_row + (i * tm) // self.seq_len


def _mod_spec(stream, layer, chunk, tm, d, row_axis=0):
    return pl.BlockSpec((None, None, 1, d),
                        lambda *g: (layer, stream.cond_row(g[row_axis], tm), 0, chunk))


def _layer_spec(arr, layer):
    zeros = (0,) * (arr.ndim - 1)
    return pl.BlockSpec((None,) + arr.shape[1:], lambda *g: (layer,) + zeros)


class _Cols:
    def __init__(self, arr, start, width):
        self.arr, self.start, self.width = arr, start, width

    def block(self, bw):
        assert self.start % bw == 0 and self.width % bw == 0
        return self.start // bw


def _rms_heads(acc, gain, n_heads):
    outs = []
    for hh in range(n_heads):
        a = acc[:, hh * HEAD_DIM:(hh + 1) * HEAD_DIM]
        ms = jnp.mean(a * a, axis=-1, keepdims=True)
        outs.append(a * lax.rsqrt(ms + RMS_EPS) * gain)
    return outs


def _rope(x, cos, sin_signed):
    return x * cos + pltpu.roll(x, HEAD_DIM // 2, 1) * sin_signed


def _split_heads(acc):
    return [acc[:, hh * HEAD_DIM:(hh + 1) * HEAD_DIM] for hh in range(acc.shape[1] // HEAD_DIM)]


def _store_heads_4d(ref, head0, blocks, seq_len):
    for hh, val in enumerate(blocks):
        for b in range(ref.shape[0]):
            ref[b, :, head0 + hh, :] = val[b * seq_len:(b + 1) * seq_len, :]


def _store_heads(ref, head0, blocks):
    for hh, val in enumerate(blocks):
        c0 = (head0 + hh) * HEAD_DIM
        ref[:, c0:c0 + HEAD_DIM] = val.astype(BF16)


def _kv32_index(region, n_tiles):
    return lambda j, i: (jnp.where(j < region, 0, jnp.where(j == region, i, n_tiles - 1)), 0, 0, 0)


def _inproj_ab_kernel(*refs, rope, sub, seq_len):
    if rope:
        x_ref, sh_ref, sc_ref, w_ref, qg_ref, kg_ref, cos_ref, sin_ref, p_ref = refs
    else:
        x_ref, sh_ref, sc_ref, w_ref, qg_ref, kg_ref, p_ref, k32_ref, v32_ref = refs
    j = pl.program_id(0)
    heads = sub // HEAD_DIM

    def modulated():
        return (x_ref[...] * (1.0 + sc_ref[...]) + sh_ref[...]).astype(BF16)

    def sub_dot(h, s):
        return _dot(h, w_ref[:, s * sub:(s + 1) * sub])

    def normed(acc, gain):
        outs = _rms_heads(acc, gain, heads)
        if rope:
            outs = [_rope(o, cos_ref[...], sin_ref[...]) for o in outs]
        return outs

    @pl.when(j == 0)
    def _():
        h = modulated()
        for s in range(N_SUB):
            _store_heads(p_ref, s * heads, normed(sub_dot(h, s), qg_ref[...] * Q_SCALE))

    @pl.when(j == 1)
    def _():
        h = modulated()
        p_ref[:, :sub] = sub_dot(h, 0).astype(BF16)
        kn = normed(sub_dot(h, 1), kg_ref[...])
        _store_heads(p_ref, heads, kn)
        vs = _split_heads(sub_dot(h, 2))
        _store_heads(p_ref, 2 * heads, vs)
        if not rope:
            _store_heads_4d(k32_ref, 0, kn, seq_len)
            _store_heads_4d(v32_ref, 0, vs, seq_len)


def _inproj_ab_call(stream, layer, kind_layer, x, mods, w, q_gain, k_gain, rope_tabs, n_kv_heads, tm=512):
    m, d = x.shape
    n = w.shape[2]
    sub = n_kv_heads * HEAD_DIM
    n_tiles = m // tm
    assert n == 2 * N_SUB * sub
    rope = rope_tabs is not None
    assert stream.seq_len % tm == 0 if rope else tm % stream.seq_len == 0
    in_specs = [
        pl.BlockSpec((tm, d), lambda j, i: (i, 0)),
        _mod_spec(stream, layer, 0, tm, d, row_axis=1),
        _mod_spec(stream, layer, 1, tm, d, row_axis=1),
        pl.BlockSpec((None, d, N_SUB * sub), lambda j, i: (kind_layer, 0, j)),
        _layer_spec(q_gain, kind_layer),
        _layer_spec(k_gain, kind_layer),
    ]
    args = [x, mods, mods, w, q_gain, k_gain]
    out_specs = [pl.BlockSpec((tm, N_SUB * sub), lambda j, i: (i, j))]
    out_shape = [jax.ShapeDtypeStruct((m, n), BF16)]
    if rope:
        tiles_per_seq = stream.seq_len // tm
        tab_spec = pl.BlockSpec((tm, HEAD_DIM), lambda j, i: (i % tiles_per_seq, 0))
        in_specs += [tab_spec, tab_spec]
        args += list(rope_tabs)
    else:
        kv_block = (tm // stream.seq_len, stream.seq_len, n_kv_heads, HEAD_DIM)
        out_specs += [pl.BlockSpec(kv_block, _kv32_index(1, n_tiles))] * 2
        out_shape += [jax.ShapeDtypeStruct((stream.n_seqs,) + kv_block[1:], F32)] * 2
    return pl.pallas_call(
        functools.partial(_inproj_ab_kernel, rope=rope, sub=sub, seq_len=stream.seq_len),
        grid=(2, n_tiles),
        in_specs=in_specs,
        out_specs=tuple(out_specs),
        out_shape=tuple(out_shape),
        compiler_params=_cparams(2),
        name="inproj_ab",
    )(*args)


def _inproj_cd_kernel(*refs, emit_kv, seq_len):
    if emit_kv:
        x_ref, sh_ref, sc_ref, w_ref, p_ref, k32_ref, v32_ref = refs
    else:
        x_ref, sh_ref, sc_ref, w_ref, p_ref = refs
    j = pl.program_id(0)
    sub = w_ref.shape[1] // N_SUB
    heads = sub // HEAD_DIM

    def region(scale, out32_ref):
        h = (x_ref[...] * (1.0 + sc_ref[...]) + sh_ref[...]).astype(BF16)
        for s in range(N_SUB):
            acc = _dot(h, w_ref[:, s * sub:(s + 1) * sub])
            if scale is not None:
                acc = acc * scale
            p_ref[:, s * sub:(s + 1) * sub] = acc.astype(BF16)
            if out32_ref is not None:
                _store_heads_4d(out32_ref, s * heads, _split_heads(acc), seq_len)

    @pl.when(j == 0)
    def _():
        region(Q_SCALE, None)

    @pl.when(j == 1)
    def _():
        region(None, k32_ref if emit_kv else None)

    @pl.when(j == 2)
    def _():
        region(None, v32_ref if emit_kv else None)

    @pl.when(j == 3)
    def _():
        region(None, None)


def _inproj_cd_call(stream, layer, kind_layer, x, mods, w, n_heads, emit_kv, tm=512):
    m, d = x.shape
    n = w.shape[2]
    hw = n_heads * HEAD_DIM
    n_tiles = m // tm
    assert n == 4 * hw and n_heads % N_SUB == 0
    in_specs = [
        pl.BlockSpec((tm, d), lambda j, i: (i, 0)),
        _mod_spec(stream, layer, 0, tm, d, row_axis=1),
        _mod_spec(stream, layer, 1, tm, d, row_axis=1),
        pl.BlockSpec((None, d, hw), lambda j, i: (kind_layer, 0, j)),
    ]
    out_specs = [pl.BlockSpec((tm, hw), lambda j, i: (i, j))]
    out_shape = [jax.ShapeDtypeStruct((m, n), BF16)]
    if emit_kv:
        kv_block = (tm // stream.seq_len, stream.seq_len, n_heads, HEAD_DIM)
        out_specs += [pl.BlockSpec(kv_block, _kv32_index(1, n_tiles)),
                      pl.BlockSpec(kv_block, _kv32_index(2, n_tiles))]
        out_shape += [jax.ShapeDtypeStruct((stream.n_seqs,) + kv_block[1:], F32)] * 2
    return pl.pallas_call(
        functools.partial(_inproj_cd_kernel, emit_kv=emit_kv, seq_len=stream.seq_len),
        grid=(4, n_tiles),
        in_specs=in_specs,
        out_specs=tuple(out_specs),
        out_shape=tuple(out_shape),
        compiler_params=_cparams(2),
        name="inproj_cd",
    )(x, mods, mods, w)


def _pool_kernel(u_ref, pw_ref, ps_ref, o_ref, *, seq_len):
    rows = u_ref.shape[0]
    cw = pw_ref.shape[1]
    t = lax.broadcasted_iota(jnp.int32, (rows, cw), 0) & (seq_len - 1)
    for g, w in enumerate(POOL_WINDOWS):
        sl = slice(g * cw, (g + 1) * cw)
        x = u_ref[:, sl].astype(F32)
        acc = jnp.zeros_like(x)
        for dlt in range(-(w // 2), w - w // 2):
            xs = x if dlt == 0 else pltpu.roll(x, (-dlt) % rows, 0)
            valid = (t + dlt >= 0) & (t + dlt <= seq_len - 1)
            acc = acc + jnp.where(valid, xs, 0.0)
        lo = jnp.clip(t - w // 2, 0, seq_len - 1)
        hi = jnp.clip(t - w // 2 + w - 1, 0, seq_len - 1)
        cnt = (hi - lo + 1).astype(F32)
        pooled = (acc / cnt - x).astype(BF16)
        y = _dot(pooled, pw_ref[g]) * ps_ref[:, sl]
        o_ref[:, sl] = y.astype(BF16)


def _pool_call(stream, kind_layer, u, pool_w, pool_scale, tr):
    m, pw = u.arr.shape[0], u.width
    ub = u.block(pw)
    assert tr % stream.seq_len == 0
    return pl.pallas_call(
        functools.partial(_pool_kernel, seq_len=stream.seq_len),
        grid=(m // tr,),
        in_specs=[
            pl.BlockSpec((tr, pw), lambda i: (i, ub)),
            _layer_spec(pool_w, kind_layer),
            _layer_spec(pool_scale, kind_layer),
        ],
        out_specs=pl.BlockSpec((tr, pw), lambda i: (i, 0)),
        out_shape=jax.ShapeDtypeStruct((m, pw), BF16),
        compiler_params=_cparams(1),
        name="pool",
    )(u.arr, pool_w, pool_scale)


def _conv_kernel(xin_ref, gb_ref, gc_ref, cw_ref, o_ref, *, seq_len):
    rows, width = xin_ref.shape
    t = lax.broadcasted_iota(jnp.int32, (rows, width), 0) & (seq_len - 1)
    u = gc_ref[...].astype(F32) * xin_ref[...].astype(F32)
    prev = jnp.where(t >= 1, pltpu.roll(u, 1, 0), 0.0)
    nxt = jnp.where(t <= seq_len - 2, pltpu.roll(u, rows - 1, 0), 0.0)
    y = cw_ref[0:1, :] * prev + cw_ref[1:2, :] * u + cw_ref[2:3, :] * nxt
    o_ref[...] = (gb_ref[...].astype(F32) * y).astype(BF16)


def _conv_call(stream, kind_layer, cin, conv_w, tr):
    m = cin.arr.shape[0]
    width = conv_w.shape[2]
    cb = cin.block(width)
    assert tr % stream.seq_len == 0 and cin.width == 3 * width
    return pl.pallas_call(
        functools.partial(_conv_kernel, seq_len=stream.seq_len),
        grid=(m // tr,),
        in_specs=[
            pl.BlockSpec((tr, width), lambda i: (i, cb)),
            pl.BlockSpec((tr, width), lambda i: (i, cb + 1)),
            pl.BlockSpec((tr, width), lambda i: (i, cb + 2)),
            _layer_spec(conv_w, kind_layer),
        ],
        out_specs=pl.BlockSpec((tr, width), lambda i: (i, 0)),
        out_shape=jax.ShapeDtypeStruct((m, width), BF16),
        compiler_params=_cparams(1),
        name="conv",
    )(cin.arr, cin.arr, cin.arr, conv_w)


def _head(ref, h, rows=slice(None)):
    return ref[rows, h * HEAD_DIM:(h + 1) * HEAD_DIM]


def _attn_seq_kernel(q_ref, k_ref, v_ref, o_ref, *, n_q_heads, group, seq_len):
    for sq in range(q_ref.shape[0] // seq_len):
        rows = slice(sq * seq_len, (sq + 1) * seq_len)
        for h in range(n_q_heads):
            kv = h // group
            s = _dot_nt(_head(q_ref, h, rows), _head(k_ref, kv, rows))
            o = _softmax_pv([s], [_head(v_ref, kv, rows)])
            o_ref[rows, h * HEAD_DIM:(h + 1) * HEAD_DIM] = o.astype(BF16)


def _attn_seq_call(stream, q, k, v, seqs_per_step=2):
    m = q.arr.shape[0]
    qw, kw = q.width, k.width
    qb, kb, vb = q.block(qw), k.block(kw), v.block(kw)
    tr = seqs_per_step * stream.seq_len
    return pl.pallas_call(
        functools.partial(_attn_seq_kernel, n_q_heads=qw // HEAD_DIM, group=qw // kw, seq_len=stream.seq_len),
        grid=(m // tr,),
        in_specs=[
            pl.BlockSpec((tr, qw), lambda b: (b, qb)),
            pl.BlockSpec((tr, kw), lambda b: (b, kb)),
            pl.BlockSpec((tr, kw), lambda b: (b, vb)),
        ],
        out_specs=pl.BlockSpec((tr, qw), lambda b: (b, 0)),
        out_shape=jax.ShapeDtypeStruct((m, qw), BF16),
        compiler_params=_cparams(1),
        name="attn_seq",
    )(q.arr, k.arr, v.arr)


def _attn_ctx_kernel(q_ref, k_ref, v_ref, kc_ref, vc_ref, o_ref):
    for g in range(q_ref.shape[1] // HEAD_DIM):
        q = _head(q_ref, g)
        o = _softmax_pv([_dot_nt(q, k_ref[...]), _dot_nt(q, kc_ref[...])], [v_ref[...], vc_ref[...]])
        o_ref[:, g * HEAD_DIM:(g + 1) * HEAD_DIM] = o.astype(BF16)


def _attn_ctx_call(stream, q, k, v, ctx_k, ctx_v, tq=256):
    m = q.arr.shape[0]
    qw = q.width
    n_kv_heads = k.width // HEAD_DIM
    gw = qw // n_kv_heads
    qb, kb, vb = q.block(gw), k.block(HEAD_DIM), v.block(HEAD_DIM)
    ls = stream.seq_len
    nb = stream.n_seqs
    nq = ls // tq
    past = ctx_k.shape[0] // nb
    return pl.pallas_call(
        _attn_ctx_kernel,
        grid=(nb, n_kv_heads, nq),
        in_specs=[
            pl.BlockSpec((tq, gw), lambda b, h, qi: (b * nq + qi, qb + h)),
            pl.BlockSpec((ls, HEAD_DIM), lambda b, h, qi: (b, kb + h)),
            pl.BlockSpec((ls, HEAD_DIM), lambda b, h, qi: (b, vb + h)),
            pl.BlockSpec((past, HEAD_DIM), lambda b, h, qi: (b, h)),
            pl.BlockSpec((past, HEAD_DIM), lambda b, h, qi: (b, h)),
        ],
        out_specs=pl.BlockSpec((tq, gw), lambda b, h, qi: (b * nq + qi, h)),
        out_shape=jax.ShapeDtypeStruct((m, qw), BF16),
        compiler_params=_cparams(3),
        name="attn_ctx",
    )(q.arr, k.arr, v.arr, ctx_k, ctx_v)


def _na_bias_kernel(rpb_ref, o_ref, *, n_rows):
    h = pl.program_id(0)
    n_dr = 2 * NA_KH - 1
    n_dc = 2 * NA_KW - 1
    pair_w = 2 * GRID_W
    lane = lax.broadcasted_iota(jnp.int32, (GRID_W, pair_w), 1)
    qcol = lax.broadcasted_iota(jnp.int32, (GRID_W, pair_w), 0)
    kcol = lane & (GRID_W - 1)
    second = lane >= GRID_W
    col_start = jnp.clip(qcol - NA_KW // 2, 0, GRID_W - NA_KW)
    col_ok = (kcol >= col_start) & (kcol < col_start + NA_KW)
    dc_idx = jnp.clip(kcol - qcol, -(NA_KW - 1), NA_KW - 1) + NA_KW - 1

    pair_tiles = []
    for dr in range(-1, n_dr):
        dr_a = min(max(dr, 0), n_dr - 1)
        dr_b = min(max(dr + 1, 0), n_dr - 1)
        tile = jnp.zeros((GRID_W, pair_w), F32)
        for dc in range(n_dc):
            base = h * (n_dr * n_dc) + dc
            val = jnp.where(second, rpb_ref[base + dr_b * n_dc], rpb_ref[base + dr_a * n_dc])
            tile = jnp.where(dc_idx == dc, val, tile)
        pair_tiles.append(jnp.where(col_ok, tile * LOG2E, MASK_VALUE))

    krow = lax.broadcasted_iota(jnp.int32, (GRID_W, NA_UROWS * GRID_W), 1) >> (GRID_W.bit_length() - 1)
    n_blocks = n_rows // NA_QROWS
    for pat, blk in enumerate((0, 1, n_blocks - 1)):
        r0 = blk * NA_QROWS
        us = min(max(r0 - NA_KH // 2, 0), n_rows - NA_UROWS)
        for i in range(NA_QROWS):
            r = r0 + i
            rs = min(max(r - NA_KH // 2, 0), n_rows - NA_KH)
            pieces = []
            for jp in range(NA_UROWS // 2):
                dr = (us + 2 * jp) - r + NA_KH - 1
                pieces.append(pair_tiles[min(max(dr, -1), n_dr - 1) + 1])
            strip = jnp.concatenate(pieces, axis=1)
            in_window = (krow >= rs - us) & (krow < rs - us + NA_KH)
            o_ref[pat, i * GRID_W:(i + 1) * GRID_W, :] = jnp.where(in_window, strip, MASK_VALUE)


def _na_bias_call(rpb, n_rows):
    heads = rpb.shape[0]
    assert NA_UROWS % 2 == 0 and n_rows // NA_QROWS >= 3
    q_tok, k_tok = NA_QROWS * GRID_W, NA_UROWS * GRID_W
    return pl.pallas_call(
        functools.partial(_na_bias_kernel, n_rows=n_rows),
        grid=(heads,),
        in_specs=[pl.BlockSpec(memory_space=pltpu.SMEM)],
        out_specs=pl.BlockSpec((3, None, q_tok, k_tok), lambda h: (0, h, 0, 0)),
        out_shape=jax.ShapeDtypeStruct((3, heads, q_tok, k_tok), F32),
        compiler_params=_cparams(1),
        name="na_bias",
    )(rpb.reshape(-1))


def _natten_kernel(q_ref, k_ref, v_ref, kc_ref, vc_ref, bias_ref, o_ref, *, n_rows):
    rb = pl.program_id(2)
    us = jnp.clip(rb * NA_QROWS - NA_KH // 2, 0, n_rows - NA_UROWS)
    start = pl.multiple_of(us * GRID_W, GRID_W * NA_QROWS)
    band = pl.ds(start, NA_UROWS * GRID_W)
    for h in range(q_ref.shape[1] // HEAD_DIM):
        q = _head(q_ref, h)
        s_nb = _dot_nt(q, _head(k_ref, h, band)) + bias_ref[h]
        s_ctx = _dot_nt(q, _head(kc_ref, h))
        o = _softmax_pv([s_nb, s_ctx], [_head(v_ref, h, band), _head(vc_ref, h)])
        o_ref[:, h * HEAD_DIM:(h + 1) * HEAD_DIM] = o.astype(BF16)


def _natten_call(stream, q, k, v, ctx_k, ctx_v, bias, heads_per_step=4):
    m, hw = q.arr.shape[0], q.width
    ls = stream.seq_len
    nb = stream.n_seqs
    n_rows = ls // GRID_W
    n_blocks = n_rows // NA_QROWS
    tq = NA_QROWS * GRID_W
    past = ctx_k.shape[0] // nb
    gw = heads_per_step * HEAD_DIM
    qb, kb, vb = q.block(gw), k.block(gw), v.block(gw)

    def pattern(rb):
        return jnp.where(rb == 0, 0, jnp.where(rb == n_blocks - 1, 2, 1))

    return pl.pallas_call(
        functools.partial(_natten_kernel, n_rows=n_rows),
        grid=(nb, hw // gw, n_blocks),
        in_specs=[
            pl.BlockSpec((tq, gw), lambda b, h, rb: (b * n_blocks + rb, qb + h)),
            pl.BlockSpec((ls, gw), lambda b, h, rb: (b, kb + h)),
            pl.BlockSpec((ls, gw), lambda b, h, rb: (b, vb + h)),
            pl.BlockSpec((past, gw), lambda b, h, rb: (b, h)),
            pl.BlockSpec((past, gw), lambda b, h, rb: (b, h)),
            pl.BlockSpec((None, heads_per_step, tq, NA_UROWS * GRID_W),
                         lambda b, h, rb: (pattern(rb), h, 0, 0)),
        ],
        out_specs=pl.BlockSpec((tq, gw), lambda b, h, rb: (b * n_blocks + rb, h)),
        out_shape=jax.ShapeDtypeStruct((m, hw), BF16),
        compiler_params=_cparams(3),
        name="natten",
    )(q.arr, k.arr, v.arr, ctx_k, ctx_v, bias)


def _outproj_kernel(a_ref, b_ref, w_ref, x_ref, gate_ref, g_ref, beta_ref, o_ref, *, alpha, n_chunks):
    ka = a_ref.shape[1]
    rc = x_ref.shape[0] // n_chunks
    for c in range(n_chunks):
        rows = slice(c * rc, (c + 1) * rc)
        y = _dot(a_ref[rows, :], w_ref[:ka, :]) + _dot(b_ref[rows, :], w_ref[ka:, :])
        o_ref[rows, :] = _layer_norm(alpha * x_ref[rows, :] + gate_ref[...] * y, g_ref[...], beta_ref[...])


def _outproj_call(stream, layer, kind_layer, a, b, w, x, mods, ln_g, ln_b, alpha, tm=512, n_chunks=2):
    m, d = x.shape
    ka, kb = a.shape[1], b.shape[1]
    return pl.pallas_call(
        functools.partial(_outproj_kernel, alpha=alpha, n_chunks=n_chunks),
        grid=(m // tm,),
        in_specs=[
            pl.BlockSpec((tm, ka), lambda i: (i, 0)),
            pl.BlockSpec((tm, kb), lambda i: (i, 0)),
            _layer_spec(w, kind_layer),
            pl.BlockSpec((tm, d), lambda i: (i, 0)),
            _mod_spec(stream, layer, 2, tm, d),
            _layer_spec(ln_g, layer),
            _layer_spec(ln_b, layer),
        ],
        out_specs=pl.BlockSpec((tm, d), lambda i: (i, 0)),
        out_shape=jax.ShapeDtypeStruct((m, d), F32),
        compiler_params=_cparams(1),
        name="outproj_ln",
    )(a, b, w, x, mods, ln_g, ln_b)


def _ffn_kernel(x_ref, sh_ref, sc_ref, gate_ref, wg_ref, wu_ref, wd_ref, g_ref, beta_ref, o_ref,
                h_scr, *, alpha, n_chunks):
    j = pl.program_id(1)
    last = pl.num_programs(1) - 1
    rc = x_ref.shape[0] // n_chunks
    chunks = [slice(c * rc, (c + 1) * rc) for c in range(n_chunks)]

    def hidden_chunk(h):
        gt = _dot(h, wg_ref[...])
        up = _dot(h, wu_ref[...])
        act = (gt * jax.nn.sigmoid(gt) * up).astype(BF16)
        return _dot(act, wd_ref[...])

    @pl.when(j == 0)
    def _():
        for rows in chunks:
            h = (x_ref[rows, :] * (1.0 + sc_ref[...]) + sh_ref[...]).astype(BF16)
            h_scr[rows, :] = h
            o_ref[rows, :] = hidden_chunk(h)

    @pl.when((j > 0) & (j < last))
    def _():
        for rows in chunks:
            o_ref[rows, :] += hidden_chunk(h_scr[rows, :])

    @pl.when(j == last)
    def _():
        for rows in chunks:
            y = o_ref[rows, :] + hidden_chunk(h_scr[rows, :])
            o_ref[rows, :] = _layer_norm(alpha * x_ref[rows, :] + gate_ref[...] * y,
                                         g_ref[...], beta_ref[...])


def _ffn_call(stream, layer, x, mods, wg, wu, wd, ln_g, ln_b, alpha, tm=1024, tf=512, n_chunks=2):
    m, d = x.shape
    f = wg.shape[2]
    return pl.pallas_call(
        functools.partial(_ffn_kernel, alpha=alpha, n_chunks=n_chunks),
        grid=(m // tm, f // tf),
        in_specs=[
            pl.BlockSpec((tm, d), lambda i, j: (i, 0)),
            _mod_spec(stream, layer, 3, tm, d),
            _mod_spec(stream, layer, 4, tm, d),
            _mod_spec(stream, layer, 5, tm, d),
            pl.BlockSpec((None, d, tf), lambda i, j: (layer, 0, j)),
            pl.BlockSpec((None, d, tf), lambda i, j: (layer, 0, j)),
            pl.BlockSpec((None, tf, d), lambda i, j: (layer, j, 0)),
            _layer_spec(ln_g, layer),
            _layer_spec(ln_b, layer),
        ],
        out_specs=pl.BlockSpec((tm, d), lambda i, j: (i, 0)),
        out_shape=jax.ShapeDtypeStruct((m, d), F32),
        scratch_shapes=[pltpu.VMEM((tm, d), BF16)],
        compiler_params=_cparams(2),
        name="ffn_ln",
    )(x, mods, mods, mods, wg, wu, wd, ln_g, ln_b)


def _rope_tables(n_tokens):
    t = jnp.arange(n_tokens)
    row = (t // GRID_W).astype(F32)
    col = (t % GRID_W).astype(F32)
    half = HEAD_DIM // 2
    inv = 1.0 / (ROPE_THETA ** (jnp.arange(0, half, 2, dtype=F32) / half))
    ang = jnp.concatenate([row[:, None] * inv, col[:, None] * inv], axis=-1)
    cos, sin = jnp.cos(ang), jnp.sin(ang)
    return jnp.concatenate([cos, cos], axis=-1), jnp.concatenate([-sin, sin], axis=-1)


def kernel(x_prompt, x_sample, cache_k_l0, cache_v_l0, cache_k_l1, cache_v_l1, cache_k_l2, cache_v_l2, cache_k_l3, cache_v_l3, c, c_ctx, w_ada, b_ada, ln1_g, ln1_b, ln2_g, ln2_b, w_in_ab, w_out_ab, pool_w, pool_scale, q_norm_g, k_norm_g, w_in_cd, w_out_cd, na_rpb, conv_w, w_ffn_gate, w_ffn_up, w_ffn_down):
    batch, seq, d = x_prompt.shape
    dec_batch, dec_seq, _ = x_sample.shape
    depth = w_ada.shape[0]
    alpha = (2 * depth) ** 0.25
    b_kv_heads = cache_k_l0.shape[2]
    c_heads = cache_k_l1.shape[2]
    pool_width = pool_w.shape[1] * pool_w.shape[2]
    caches = [(cache_k_l0, cache_v_l0), (cache_k_l1, cache_v_l1),
              (cache_k_l2, cache_v_l2), (cache_k_l3, cache_v_l3)]

    prompt = _Stream(batch * seq, seq, 0, False)
    sample = _Stream(dec_batch * dec_seq, dec_seq, 1, True)
    assert 1 + dec_batch <= N_COND_ROWS

    cond = jnp.concatenate(
        [c_ctx[None, :], c, jnp.zeros((N_COND_ROWS - 1 - dec_batch, d), F32)], axis=0)
    mods = _ada_call(cond, w_ada, b_ada).reshape(depth, N_COND_ROWS, 1, 6 * d)
    rope_tabs = _rope_tables(dec_seq)

    kvw = b_kv_heads * HEAD_DIM
    qw = w_in_ab.shape[2] - pool_width - 2 * kvw
    w_in_ab = jnp.concatenate(
        [w_in_ab[:, :, pool_width:pool_width + qw], w_in_ab[:, :, :pool_width],
         w_in_ab[:, :, pool_width + qw:]], axis=2)
    w_in_ab, w_out_ab, w_in_cd, w_out_cd, pool_w, w_ffn_gate, w_ffn_up, w_ffn_down = (
        w.astype(BF16) for w in (w_in_ab, w_out_ab, w_in_cd, w_out_cd, pool_w,
                                 w_ffn_gate, w_ffn_up, w_ffn_down))
    ln1_g, ln1_b, ln2_g, ln2_b, pool_scale, q_norm_g, k_norm_g = (
        v[:, None, :] for v in (ln1_g, ln1_b, ln2_g, ln2_b, pool_scale, q_norm_g, k_norm_g))

    def split_ab(p):
        return (_Cols(p, 0, qw), _Cols(p, qw, pool_width),
                _Cols(p, qw + pool_width, kvw), _Cols(p, qw + pool_width + kvw, kvw))

    def split_cd(p):
        hw = c_heads * HEAD_DIM
        return _Cols(p, 0, hw), _Cols(p, hw, hw), _Cols(p, 2 * hw, hw), _Cols(p, 3 * hw, hw)

    xp = x_prompt.reshape(batch * seq, d)
    xs = x_sample.reshape(dec_batch * dec_seq, d)
    new_state = []
    for i in range(depth):
        jj = i // 2
        ck, cv = caches[i]
        ck2 = ck.reshape(dec_batch * ck.shape[1], ck.shape[2] * HEAD_DIM).astype(BF16)
        cv2 = cv.reshape(dec_batch * cv.shape[1], cv.shape[2] * HEAD_DIM).astype(BF16)
        if i % 2 == 0:
            pp, k32, v32 = _inproj_ab_call(
                prompt, i, jj, xp, mods, w_in_ab, q_norm_g, k_norm_g, None, b_kv_heads)
            (ps,) = _inproj_ab_call(
                sample, i, jj, xs, mods, w_in_ab, q_norm_g, k_norm_g, rope_tabs, b_kv_heads)
            qp, up, kp, vp = split_ab(pp)
            qs, us, ks, vs = split_ab(ps)
            mix_p = (_pool_call(prompt, jj, up, pool_w, pool_scale, tr=1024),
                     _attn_seq_call(prompt, qp, kp, vp))
            mix_s = (_pool_call(sample, jj, us, pool_w, pool_scale, tr=dec_seq),
                     _attn_ctx_call(sample, qs, ks, vs, ck2, cv2))
            w_out = w_out_ab
        else:
            pp, k32, v32 = _inproj_cd_call(prompt, i, jj, xp, mods, w_in_cd, c_heads, True)
            (ps,) = _inproj_cd_call(sample, i, jj, xs, mods, w_in_cd, c_heads, False)
            qp, kp, vp, cp = split_cd(pp)
            qs, ks, vs, cs = split_cd(ps)
            bias = _na_bias_call(na_rpb[jj], dec_seq // GRID_W)
            mix_p = (_attn_seq_call(prompt, qp, kp, vp),
                     _conv_call(prompt, jj, cp, conv_w, tr=1024))
            mix_s = (_natten_call(sample, qs, ks, vs, ck2, cv2, bias),
                     _conv_call(sample, jj, cs, conv_w, tr=dec_seq))
            w_out = w_out_cd
        new_state += [k32, v32]
        xp = _outproj_call(prompt, i, jj, mix_p[0], mix_p[1], w_out, xp, mods, ln1_g, ln1_b, alpha)
        xs = _outproj_call(sample, i, jj, mix_s[0], mix_s[1], w_out, xs, mods, ln1_g, ln1_b, alpha)
        xp = _ffn_call(prompt, i, xp, mods, w_ffn_gate, w_ffn_up, w_ffn_down, ln2_g, ln2_b, alpha)
        xs = _ffn_call(sample, i, xs, mods, w_ffn_gate, w_ffn_up, w_ffn_down, ln2_g, ln2_b, alpha)
    return (xp.reshape(batch, seq, d), xs.reshape(dec_batch, dec_seq, d), *new_state)
```

```python
import functools
import math

import jax
import jax.numpy as jnp
from jax import lax
from jax.experimental import pallas as pl
from jax.experimental.pallas import tpu as pltpu

F32 = jnp.float32
BF16 = jnp.bfloat16

GRID_W = 64
HEAD_DIM = 128
POOL_WINDOWS = (2, 4, 8, 16)
NA_KH = 8
NA_KW = 16
ROPE_THETA = 10000.0
LN_EPS = 1e-5
RMS_EPS = 1e-6
LOG2E = math.log2(math.e)
Q_SCALE = HEAD_DIM ** -0.5 * LOG2E

NA_QROWS = 4
NA_UROWS = NA_KH + NA_QROWS
MASK_VALUE = -1e30

N_COND_ROWS = 8
N_SUB = 3
VMEM_LIMIT = 56 * 1024 * 1024


def _cparams(n_grid_dims):
    return pltpu.CompilerParams(dimension_semantics=("arbitrary",) * n_grid_dims,
                                vmem_limit_bytes=VMEM_LIMIT)


def _layer_norm(z, g, b):
    mu = jnp.mean(z, axis=-1, keepdims=True)
    zc = z - mu
    var = jnp.mean(zc * zc, axis=-1, keepdims=True)
    return zc * lax.rsqrt(var + LN_EPS) * g + b


def _dot(a, b):
    return jnp.dot(a, b, preferred_element_type=F32)


def _dot_nt(a, b):
    return lax.dot_general(a, b, (((1,), (1,)), ((), ())), preferred_element_type=F32)


def _softmax_pv(score_blocks, value_blocks):
    mx = functools.reduce(jnp.maximum, [jnp.max(s, axis=-1, keepdims=True) for s in score_blocks])
    probs = [jnp.exp2(s - mx) for s in score_blocks]
    den = sum(jnp.sum(p, axis=-1, keepdims=True) for p in probs)
    out = sum(_dot(p.astype(BF16), v) for p, v in zip(probs, value_blocks))
    return out / den


def _ada_kernel(cond_ref, w_ref, b_ref, o_ref):
    c = cond_ref[...]
    s = (c * jax.nn.sigmoid(c)).astype(BF16)
    o_ref[...] = _dot(s, w_ref[...].astype(BF16)) + b_ref[...]


def _ada_call(cond, w_ada, b_ada, tn=1024):
    depth, d, n = w_ada.shape
    return pl.pallas_call(
        _ada_kernel,
        grid=(depth, n // tn),
        in_specs=[
            pl.BlockSpec((N_COND_ROWS, d), lambda l, j: (0, 0)),
            pl.BlockSpec((None, d, tn), lambda l, j: (l, 0, j)),
            pl.BlockSpec((None, 1, tn), lambda l, j: (l, 0, j)),
        ],
        out_specs=pl.BlockSpec((None, N_COND_ROWS, tn), lambda l, j: (l, 0, j)),
        out_shape=jax.ShapeDtypeStruct((depth, N_COND_ROWS, n), F32),
        compiler_params=_cparams(2),
        name="ada",
    )(cond, w_ada, b_ada.reshape(depth, 1, n))


class _Stream:
    def __init__(self, rows, seq_len, first_cond_row, per_seq_cond):
        self.rows = rows
        self.seq_len = seq_len
        self.n_seqs = rows // seq_len
        self.first_cond_row = first_cond_row
        self.per_seq_cond = per_seq_cond

    def cond_row(self, i, tm):
        if not self.per_seq_cond:
            return self.first_cond_row
        return self.first_cond_row + (i * tm) // self.seq_len


def _mod_spec(stream, layer, chunk, tm, d, row_axis=0):
    return pl.BlockSpec((None, None, 1, d),
                        lambda *g: (layer, stream.cond_row(g[row_axis], tm), 0, chunk))


def _layer_spec(arr, layer):
    zeros = (0,) * (arr.ndim - 1)
    return pl.BlockSpec((None,) + arr.shape[1:], lambda *g: (layer,) + zeros)


class _Cols:
    def __init__(self, arr, start, width):
        self.arr, self.start, self.width = arr, start, width

    def block(self, bw):
        assert self.start % bw == 0 and self.width % bw == 0
        return self.start // bw


def _rms_heads(acc, gain, n_heads):
    outs = []
    for hh in range(n_heads):
        a = acc[:, hh * HEAD_DIM:(hh + 1) * HEAD_DIM]
        ms = jnp.mean(a * a, axis=-1, keepdims=True)
        outs.append(a * lax.rsqrt(ms + RMS_EPS) * gain)
    return outs


def _rope(x, cos, sin_signed):
    return x * cos + pltpu.roll(x, HEAD_DIM // 2, 1) * sin_signed


def _split_heads(acc):
    return [acc[:, hh * HEAD_DIM:(hh + 1) * HEAD_DIM] for hh in range(acc.shape[1] // HEAD_DIM)]


def _store_heads_4d(ref, head0, blocks, seq_len):
    for hh, val in enumerate(blocks):
        for b in range(ref.shape[0]):
            ref[b, :, head0 + hh, :] = val[b * seq_len:(b + 1) * seq_len, :]


def _store_heads(ref, head0, blocks):
    for hh, val in enumerate(blocks):
        c0 = (head0 + hh) * HEAD_DIM
        ref[:, c0:c0 + HEAD_DIM] = val.astype(BF16)


def _kv32_index(region, n_tiles):
    return lambda j, i: (jnp.where(j < region, 0, jnp.where(j == region, i, n_tiles - 1)), 0, 0, 0)


def _inproj_ab_kernel(*refs, rope, sub, seq_len):
    if rope:
        x_ref, sh_ref, sc_ref, w_ref, qg_ref, kg_ref, cos_ref, sin_ref, p_ref = refs
    else:
        x_ref, sh_ref, sc_ref, w_ref, qg_ref, kg_ref, p_ref, k32_ref, v32_ref = refs
    j = pl.program_id(0)
    heads = sub // HEAD_DIM

    def modulated():
        return (x_ref[...] * (1.0 + sc_ref[...]) + sh_ref[...]).astype(BF16)

    def sub_dot(h, s):
        return _dot(h, w_ref[:, s * sub:(s + 1) * sub])

    def normed(acc, gain):
        outs = _rms_heads(acc, gain, heads)
        if rope:
            outs = [_rope(o, cos_ref[...], sin_ref[...]) for o in outs]
        return outs

    @pl.when(j == 0)
    def _():
        h = modulated()
        for s in range(N_SUB):
            _store_heads(p_ref, s * heads, normed(sub_dot(h, s), qg_ref[...] * Q_SCALE))

    @pl.when(j == 1)
    def _():
        h = modulated()
        p_ref[:, :sub] = sub_dot(h, 0).astype(BF16)
        kn = normed(sub_dot(h, 1), kg_ref[...])
        _store_heads(p_ref, heads, kn)
        vs = _split_heads(sub_dot(h, 2))
        _store_heads(p_ref, 2 * heads, vs)
        if not rope:
            _store_heads_4d(k32_ref, 0, kn, seq_len)
            _store_heads_4d(v32_ref, 0, vs, seq_len)


def _inproj_ab_call(stream, layer, kind_layer, x, mods, w, q_gain, k_gain, rope_tabs, n_kv_heads, tm=512):
    m, d = x.shape
    n = w.shape[2]
    sub = n_kv_heads * HEAD_DIM
    n_tiles = m // tm
    assert n == 2 * N_SUB * sub
    rope = rope_tabs is not None
    assert stream.seq_len % tm == 0 if rope else tm % stream.seq_len == 0
    in_specs = [
        pl.BlockSpec((tm, d), lambda j, i: (i, 0)),
        _mod_spec(stream, layer, 0, tm, d, row_axis=1),
        _mod_spec(stream, layer, 1, tm, d, row_axis=1),
        pl.BlockSpec((None, d, N_SUB * sub), lambda j, i: (kind_layer, 0, j)),
        _layer_spec(q_gain, kind_layer),
        _layer_spec(k_gain, kind_layer),
    ]
    args = [x, mods, mods, w, q_gain, k_gain]
    out_specs = [pl.BlockSpec((tm, N_SUB * sub), lambda j, i: (i, j))]
    out_shape = [jax.ShapeDtypeStruct((m, n), BF16)]
    if rope:
        tiles_per_seq = stream.seq_len // tm
        tab_spec = pl.BlockSpec((tm, HEAD_DIM), lambda j, i: (i % tiles_per_seq, 0))
        in_specs += [tab_spec, tab_spec]
        args += list(rope_tabs)
    else:
        kv_block = (tm // stream.seq_len, stream.seq_len, n_kv_heads, HEAD_DIM)
        out_specs += [pl.BlockSpec(kv_block, _kv32_index(1, n_tiles))] * 2
        out_shape += [jax.ShapeDtypeStruct((stream.n_seqs,) + kv_block[1:], F32)] * 2
    return pl.pallas_call(
        functools.partial(_inproj_ab_kernel, rope=rope, sub=sub, seq_len=stream.seq_len),
        grid=(2, n_tiles),
        in_specs=in_specs,
        out_specs=tuple(out_specs),
        out_shape=tuple(out_shape),
        compiler_params=_cparams(2),
        name="inproj_ab",
    )(*args)


def _inproj_cd_kernel(*refs, emit_kv, seq_len):
    if emit_kv:
        x_ref, sh_ref, sc_ref, w_ref, p_ref, k32_ref, v32_ref = refs
    else:
        x_ref, sh_ref, sc_ref, w_ref, p_ref = refs
    j = pl.program_id(0)
    sub = w_ref.shape[1] // N_SUB
    heads = sub // HEAD_DIM

    def region(scale, out32_ref):
        h = (x_ref[...] * (1.0 + sc_ref[...]) + sh_ref[...]).astype(BF16)
        for s in range(N_SUB):
            acc = _dot(h, w_ref[:, s * sub:(s + 1) * sub])
            if scale is not None:
                acc = acc * scale
            p_ref[:, s * sub:(s + 1) * sub] = acc.astype(BF16)
            if out32_ref is not None:
                _store_heads_4d(out32_ref, s * heads, _split_heads(acc), seq_len)

    @pl.when(j == 0)
    def _():
        region(Q_SCALE, None)

    @pl.when(j == 1)
    def _():
        region(None, k32_ref if emit_kv else None)

    @pl.when(j == 2)
    def _():
        region(None, v32_ref if emit_kv else None)

    @pl.when(j == 3)
    def _():
        region(None, None)


def _inproj_cd_call(stream, layer, kind_layer, x, mods, w, n_heads, emit_kv, tm=512):
    m, d = x.shape
    n = w.shape[2]
    hw = n_heads * HEAD_DIM
    n_tiles = m // tm
    assert n == 4 * hw and n_heads % N_SUB == 0
    in_specs = [
        pl.BlockSpec((tm, d), lambda j, i: (i, 0)),
        _mod_spec(stream, layer, 0, tm, d, row_axis=1),
        _mod_spec(stream, layer, 1, tm, d, row_axis=1),
        pl.BlockSpec((None, d, hw), lambda j, i: (kind_layer, 0, j)),
    ]
    out_specs = [pl.BlockSpec((tm, hw), lambda j, i: (i, j))]
    out_shape = [jax.ShapeDtypeStruct((m, n), BF16)]
    if emit_kv:
        kv_block = (tm // stream.seq_len, stream.seq_len, n_heads, HEAD_DIM)
        out_specs += [pl.BlockSpec(kv_block, _kv32_index(1, n_tiles)),
                      pl.BlockSpec(kv_block, _kv32_index(2, n_tiles))]
        out_shape += [jax.ShapeDtypeStruct((stream.n_seqs,) + kv_block[1:], F32)] * 2
    return pl.pallas_call(
        functools.partial(_inproj_cd_kernel, emit_kv=emit_kv, seq_len=stream.seq_len),
        grid=(4, n_tiles),
        in_specs=in_specs,
        out_specs=tuple(out_specs),
        out_shape=tuple(out_shape),
        compiler_params=_cparams(2),
        name="inproj_cd",
    )(x, mods, mods, w)


def _pool_kernel(u_ref, pw_ref, ps_ref, o_ref, *, seq_len):
    rows = u_ref.shape[0]
    cw = pw_ref.shape[1]
    t = lax.broadcasted_iota(jnp.int32, (rows, cw), 0) & (seq_len - 1)
    for g, w in enumerate(POOL_WINDOWS):
        sl = slice(g * cw, (g + 1) * cw)
        x = u_ref[:, sl].astype(F32)
        acc = jnp.zeros_like(x)
        for dlt in range(-(w // 2), w - w // 2):
            xs = x if dlt == 0 else pltpu.roll(x, (-dlt) % rows, 0)
            valid = (t + dlt >= 0) & (t + dlt <= seq_len - 1)
            acc = acc + jnp.where(valid, xs, 0.0)
        lo = jnp.clip(t - w // 2, 0, seq_len - 1)
        hi = jnp.clip(t - w // 2 + w - 1, 0, seq_len - 1)
        cnt = (hi - lo + 1).astype(F32)
        pooled = (acc / cnt - x).astype(BF16)
        y = _dot(pooled, pw_ref[g]) * ps_ref[:, sl]
        o_ref[:, sl] = y.astype(BF16)


def _pool_call(stream, kind_layer, u, pool_w, pool_scale, tr):
    m, pw = u.arr.shape[0], u.width
    ub = u.block(pw)
    assert tr % stream.seq_len == 0
    return pl.pallas_call(
        functools.partial(_pool_kernel, seq_len=stream.seq_len),
        grid=(m // tr,),
        in_specs=[
            pl.BlockSpec((tr, pw), lambda i: (i, ub)),
            _layer_spec(pool_w, kind_layer),
            _layer_spec(pool_scale, kind_layer),
        ],
        out_specs=pl.BlockSpec((tr, pw), lambda i: (i, 0)),
        out_shape=jax.ShapeDtypeStruct((m, pw), BF16),
        compiler_params=_cparams(1),
        name="pool",
    )(u.arr, pool_w, pool_scale)


def _conv_kernel(xin_ref, gb_ref, gc_ref, cw_ref, o_ref, *, seq_len):
    rows, width = xin_ref.shape
    t = lax.broadcasted_iota(jnp.int32, (rows, width), 0) & (seq_len - 1)
    u = gc_ref[...].astype(F32) * xin_ref[...].astype(F32)
    prev = jnp.where(t >= 1, pltpu.roll(u, 1, 0), 0.0)
    nxt = jnp.where(t <= seq_len - 2, pltpu.roll(u, rows - 1, 0), 0.0)
    y = cw_ref[0:1, :] * prev + cw_ref[1:2, :] * u + cw_ref[2:3, :] * nxt
    o_ref[...] = (gb_ref[...].astype(F32) * y).astype(BF16)


def _conv_call(stream, kind_layer, cin, conv_w, tr):
    m = cin.arr.shape[0]
    width = conv_w.shape[2]
    cb = cin.block(width)
    assert tr % stream.seq_len == 0 and cin.width == 3 * width
    return pl.pallas_call(
        functools.partial(_conv_kernel, seq_len=stream.seq_len),
        grid=(m // tr,),
        in_specs=[
            pl.BlockSpec((tr, width), lambda i: (i, cb)),
            pl.BlockSpec((tr, width), lambda i: (i, cb + 1)),
            pl.BlockSpec((tr, width), lambda i: (i, cb + 2)),
            _layer_spec(conv_w, kind_layer),
        ],
        out_specs=pl.BlockSpec((tr, width), lambda i: (i, 0)),
        out_shape=jax.ShapeDtypeStruct((m, width), BF16),
        compiler_params=_cparams(1),
        name="conv",
    )(cin.arr, cin.arr, cin.arr, conv_w)


def _head(ref, h, rows=slice(None)):
    return ref[rows, h * HEAD_DIM:(h + 1) * HEAD_DIM]


def _attn_seq_kernel(q_ref, k_ref, v_ref, o_ref, *, n_q_heads, group, seq_len):
    for sq in range(q_ref.shape[0] // seq_len):
        rows = slice(sq * seq_len, (sq + 1) * seq_len)
        for h in range(n_q_heads):
            kv = h // group
            s = _dot_nt(_head(q_ref, h, rows), _head(k_ref, kv, rows))
            o = _softmax_pv([s], [_head(v_ref, kv, rows)])
            o_ref[rows, h * HEAD_DIM:(h + 1) * HEAD_DIM] = o.astype(BF16)


def _attn_seq_call(stream, q, k, v, seqs_per_step=4):
    m = q.arr.shape[0]
    qw, kw = q.width, k.width
    qb, kb, vb = q.block(qw), k.block(kw), v.block(kw)
    tr = seqs_per_step * stream.seq_len
    return pl.pallas_call(
        functools.partial(_attn_seq_kernel, n_q_heads=qw // HEAD_DIM, group=qw // kw, seq_len=stream.seq_len),
        grid=(m // tr,),
        in_specs=[
            pl.BlockSpec((tr, qw), lambda b: (b, qb)),
            pl.BlockSpec((tr, kw), lambda b: (b, kb)),
            pl.BlockSpec((tr, kw), lambda b: (b, vb)),
        ],
        out_specs=pl.BlockSpec((tr, qw), lambda b: (b, 0)),
        out_shape=jax.ShapeDtypeStruct((m, qw), BF16),
        compiler_params=_cparams(1),
        name="attn_seq",
    )(q.arr, k.arr, v.arr)


def _attn_ctx_kernel(q_ref, k_ref, v_ref, kc_ref, vc_ref, o_ref):
    for g in range(q_ref.shape[1] // HEAD_DIM):
        q = _head(q_ref, g)
        o = _softmax_pv([_dot_nt(q, k_ref[...]), _dot_nt(q, kc_ref[...])], [v_ref[...], vc_ref[...]])
        o_ref[:, g * HEAD_DIM:(g + 1) * HEAD_DIM] = o.astype(BF16)


def _attn_ctx_call(stream, q, k, v, ctx_k, ctx_v, tq=512):
    m = q.arr.shape[0]
    qw = q.width
    n_kv_heads = k.width // HEAD_DIM
    gw = qw // n_kv_heads
    qb, kb, vb = q.block(gw), k.block(HEAD_DIM), v.block(HEAD_DIM)
    ls = stream.seq_len
    nb = stream.n_seqs
    nq = ls // tq
    past = ctx_k.shape[0] // nb
    return pl.pallas_call(
        _attn_ctx_kernel,
        grid=(nb, n_kv_heads, nq),
        in_specs=[
            pl.BlockSpec((tq, gw), lambda b, h, qi: (b * nq + qi, qb + h)),
            pl.BlockSpec((ls, HEAD_DIM), lambda b, h, qi: (b, kb + h)),
            pl.BlockSpec((ls, HEAD_DIM), lambda b, h, qi: (b, vb + h)),
            pl.BlockSpec((past, HEAD_DIM), lambda b, h, qi: (b, h)),
            pl.BlockSpec((past, HEAD_DIM), lambda b, h, qi: (b, h)),
        ],
        out_specs=pl.BlockSpec((tq, gw), lambda b, h, qi: (b * nq + qi, h)),
        out_shape=jax.ShapeDtypeStruct((m, qw), BF16),
        compiler_params=_cparams(3),
        name="attn_ctx",
    )(q.arr, k.arr, v.arr, ctx_k, ctx_v)


def _na_bias_kernel(rpb_ref, o_ref, *, n_rows):
    h = pl.program_id(0)
    n_dr = 2 * NA_KH - 1
    n_dc = 2 * NA_KW - 1
    pair_w = 2 * GRID_W
    lane = lax.broadcasted_iota(jnp.int32, (GRID_W, pair_w), 1)
    qcol = lax.broadcasted_iota(jnp.int32, (GRID_W, pair_w), 0)
    kcol = lane & (GRID_W - 1)
    second = lane >= GRID_W
    col_start = jnp.clip(qcol - NA_KW // 2, 0, GRID_W - NA_KW)
    col_ok = (kcol >= col_start) & (kcol < col_start + NA_KW)
    dc_idx = jnp.clip(kcol - qcol, -(NA_KW - 1), NA_KW - 1) + NA_KW - 1

    pair_tiles = []
    for dr in range(-1, n_dr):
        dr_a = min(max(dr, 0), n_dr - 1)
        dr_b = min(max(dr + 1, 0), n_dr - 1)
        tile = jnp.zeros((GRID_W, pair_w), F32)
        for dc in range(n_dc):
            base = h * (n_dr * n_dc) + dc
            val = jnp.where(second, rpb_ref[base + dr_b * n_dc], rpb_ref[base + dr_a * n_dc])
            tile = jnp.where(dc_idx == dc, val, tile)
        pair_tiles.append(jnp.where(col_ok, tile * LOG2E, MASK_VALUE))

    krow = lax.broadcasted_iota(jnp.int32, (GRID_W, NA_UROWS * GRID_W), 1) >> (GRID_W.bit_length() - 1)
    n_blocks = n_rows // NA_QROWS
    for pat, blk in enumerate((0, 1, n_blocks - 1)):
        r0 = blk * NA_QROWS
        us = min(max(r0 - NA_KH // 2, 0), n_rows - NA_UROWS)
        for i in range(NA_QROWS):
            r = r0 + i
            rs = min(max(r - NA_KH // 2, 0), n_rows - NA_KH)
            pieces = []
            for jp in range(NA_UROWS // 2):
                dr = (us + 2 * jp) - r + NA_KH - 1
                pieces.append(pair_tiles[min(max(dr, -1), n_dr - 1) + 1])
            strip = jnp.concatenate(pieces, axis=1)
            in_window = (krow >= rs - us) & (krow < rs - us + NA_KH)
            o_ref[pat, i * GRID_W:(i + 1) * GRID_W, :] = jnp.where(in_window, strip, MASK_VALUE)


def _na_bias_call(rpb, n_rows):
    heads = rpb.shape[0]
    assert NA_UROWS % 2 == 0 and n_rows // NA_QROWS >= 3
    q_tok, k_tok = NA_QROWS * GRID_W, NA_UROWS * GRID_W
    return pl.pallas_call(
        functools.partial(_na_bias_kernel, n_rows=n_rows),
        grid=(heads,),
        in_specs=[pl.BlockSpec(memory_space=pltpu.SMEM)],
        out_specs=pl.BlockSpec((3, None, q_tok, k_tok), lambda h: (0, h, 0, 0)),
        out_shape=jax.ShapeDtypeStruct((3, heads, q_tok, k_tok), F32),
        compiler_params=_cparams(1),
        name="na_bias",
    )(rpb.reshape(-1))


def _natten_kernel(q_ref, k_ref, v_ref, kc_ref, vc_ref, bias_ref, o_ref, *, n_rows):
    rb = pl.program_id(2)
    us = jnp.clip(rb * NA_QROWS - NA_KH // 2, 0, n_rows - NA_UROWS)
    start = pl.multiple_of(us * GRID_W, GRID_W * NA_QROWS)
    band = pl.ds(start, NA_UROWS * GRID_W)
    for h in range(q_ref.shape[1] // HEAD_DIM):
        q = _head(q_ref, h)
        s_nb = _dot_nt(q, _head(k_ref, h, band)) + bias_ref[h]
        s_ctx = _dot_nt(q, _head(kc_ref, h))
        o = _softmax_pv([s_nb, s_ctx], [_head(v_ref, h, band), _head(vc_ref, h)])
        o_ref[:, h * HEAD_DIM:(h + 1) * HEAD_DIM] = o.astype(BF16)


def _natten_call(stream, q, k, v, ctx_k, ctx_v, bias, heads_per_step=6):
    m, hw = q.arr.shape[0], q.width
    ls = stream.seq_len
    nb = stream.n_seqs
    n_rows = ls // GRID_W
    n_blocks = n_rows // NA_QROWS
    tq = NA_QROWS * GRID_W
    past = ctx_k.shape[0] // nb
    gw = heads_per_step * HEAD_DIM
    qb, kb, vb = q.block(gw), k.block(gw), v.block(gw)

    def pattern(rb):
        return jnp.where(rb == 0, 0, jnp.where(rb == n_blocks - 1, 2, 1))

    return pl.pallas_call(
        functools.partial(_natten_kernel, n_rows=n_rows),
        grid=(nb, hw // gw, n_blocks),
        in_specs=[
            pl.BlockSpec((tq, gw), lambda b, h, rb: (b * n_blocks + rb, qb + h)),
            pl.BlockSpec((ls, gw), lambda b, h, rb: (b, kb + h)),
            pl.BlockSpec((ls, gw), lambda b, h, rb: (b, vb + h)),
            pl.BlockSpec((past, gw), lambda b, h, rb: (b, h)),
            pl.BlockSpec((past, gw), lambda b, h, rb: (b, h)),
            pl.BlockSpec((None, heads_per_step, tq, NA_UROWS * GRID_W),
                         lambda b, h, rb: (pattern(rb), h, 0, 0)),
        ],
        out_specs=pl.BlockSpec((tq, gw), lambda b, h, rb: (b * n_blocks + rb, h)),
        out_shape=jax.ShapeDtypeStruct((m, hw), BF16),
        compiler_params=_cparams(3),
        name="natten",
    )(q.arr, k.arr, v.arr, ctx_k, ctx_v, bias)


def _outproj_kernel(a_ref, b_ref, w_ref, x_ref, gate_ref, g_ref, beta_ref, o_ref, *, alpha, n_chunks):
    ka = a_ref.shape[1]
    rc = x_ref.shape[0] // n_chunks
    for c in range(n_chunks):
        rows = slice(c * rc, (c + 1) * rc)
        y = _dot(a_ref[rows, :], w_ref[:ka, :]) + _dot(b_ref[rows, :], w_ref[ka:, :])
        o_ref[rows, :] = _layer_norm(alpha * x_ref[rows, :] + gate_ref[...] * y, g_ref[...], beta_ref[...])


def _outproj_call(stream, layer, kind_layer, a, b, w, x, mods, ln_g, ln_b, alpha, tm=512, n_chunks=2):
    m, d = x.shape
    ka, kb = a.shape[1], b.shape[1]
    return pl.pallas_call(
        functools.partial(_outproj_kernel, alpha=alpha, n_chunks=n_chunks),
        grid=(m // tm,),
        in_specs=[
            pl.BlockSpec((tm, ka), lambda i: (i, 0)),
            pl.BlockSpec((tm, kb), lambda i: (i, 0)),
            _layer_spec(w, kind_layer),
            pl.BlockSpec((tm, d), lambda i: (i, 0)),
            _mod_spec(stream, layer, 2, tm, d),
            _layer_spec(ln_g, layer),
            _layer_spec(ln_b, layer),
        ],
        out_specs=pl.BlockSpec((tm, d), lambda i: (i, 0)),
        out_shape=jax.ShapeDtypeStruct((m, d), F32),
        compiler_params=_cparams(1),
        name="outproj_ln",
    )(a, b, w, x, mods, ln_g, ln_b)


def _ffn_kernel(x_ref, sh_ref, sc_ref, gate_ref, wg_ref, wu_ref, wd_ref, g_ref, beta_ref, o_ref,
                h_scr, *, alpha, n_chunks):
    j = pl.program_id(1)
    last = pl.num_programs(1) - 1
    rc = x_ref.shape[0] // n_chunks
    chunks = [slice(c * rc, (c + 1) * rc) for c in range(n_chunks)]

    def hidden_chunk(h):
        gt = _dot(h, wg_ref[...])
        up = _dot(h, wu_ref[...])
        act = (gt * jax.nn.sigmoid(gt) * up).astype(BF16)
        return _dot(act, wd_ref[...])

    @pl.when(j == 0)
    def _():
        for rows in chunks:
            h = (x_ref[rows, :] * (1.0 + sc_ref[...]) + sh_ref[...]).astype(BF16)
            h_scr[rows, :] = h
            o_ref[rows, :] = hidden_chunk(h)

    @pl.when((j > 0) & (j < last))
    def _():
        for rows in chunks:
            o_ref[rows, :] += hidden_chunk(h_scr[rows, :])

    @pl.when(j == last)
    def _():
        for rows in chunks:
            y = o_ref[rows, :] + hidden_chunk(h_scr[rows, :])
            o_ref[rows, :] = _layer_norm(alpha * x_ref[rows, :] + gate_ref[...] * y,
                                         g_ref[...], beta_ref[...])


def _ffn_call(stream, layer, x, mods, wg, wu, wd, ln_g, ln_b, alpha, tm=1024, tf=512, n_chunks=2):
    m, d = x.shape
    f = wg.shape[2]
    return pl.pallas_call(
        functools.partial(_ffn_kernel, alpha=alpha, n_chunks=n_chunks),
        grid=(m // tm, f // tf),
        in_specs=[
            pl.BlockSpec((tm, d), lambda i, j: (i, 0)),
            _mod_spec(stream, layer, 3, tm, d),
            _mod_spec(stream, layer, 4, tm, d),
            _mod_spec(stream, layer, 5, tm, d),
            pl.BlockSpec((None, d, tf), lambda i, j: (layer, 0, j)),
            pl.BlockSpec((None, d, tf), lambda i, j: (layer, 0, j)),
            pl.BlockSpec((None, tf, d), lambda i, j: (layer, j, 0)),
            _layer_spec(ln_g, layer),
            _layer_spec(ln_b, layer),
        ],
        out_specs=pl.BlockSpec((tm, d), lambda i, j: (i, 0)),
        out_shape=jax.ShapeDtypeStruct((m, d), F32),
        scratch_shapes=[pltpu.VMEM((tm, d), BF16)],
        compiler_params=_cparams(2),
        name="ffn_ln",
    )(x, mods, mods, mods, wg, wu, wd, ln_g, ln_b)


def _rope_tables(n_tokens):
    t = jnp.arange(n_tokens)
    row = (t // GRID_W).astype(F32)
    col = (t % GRID_W).astype(F32)
    half = HEAD_DIM // 2
    inv = 1.0 / (ROPE_THETA ** (jnp.arange(0, half, 2, dtype=F32) / half))
    ang = jnp.concatenate([row[:, None] * inv, col[:, None] * inv], axis=-1)
    cos, sin = jnp.cos(ang), jnp.sin(ang)
    return jnp.concatenate([cos, cos], axis=-1), jnp.concatenate([-sin, sin], axis=-1)


def kernel(x_prompt, x_sample, cache_k_l0, cache_v_l0, cache_k_l1, cache_v_l1, cache_k_l2, cache_v_l2, cache_k_l3, cache_v_l3, c, c_ctx, w_ada, b_ada, ln1_g, ln1_b, ln2_g, ln2_b, w_in_ab, w_out_ab, pool_w, pool_scale, q_norm_g, k_norm_g, w_in_cd, w_out_cd, na_rpb, conv_w, w_ffn_gate, w_ffn_up, w_ffn_down):
    batch, seq, d = x_prompt.shape
    dec_batch, dec_seq, _ = x_sample.shape
    depth = w_ada.shape[0]
    alpha = (2 * depth) ** 0.25
    b_kv_heads = cache_k_l0.shape[2]
    c_heads = cache_k_l1.shape[2]
    pool_width = pool_w.shape[1] * pool_w.shape[2]
    caches = [(cache_k_l0, cache_v_l0), (cache_k_l1, cache_v_l1),
              (cache_k_l2, cache_v_l2), (cache_k_l3, cache_v_l3)]

    prompt = _Stream(batch * seq, seq, 0, False)
    sample = _Stream(dec_batch * dec_seq, dec_seq, 1, True)
    assert 1 + dec_batch <= N_COND_ROWS

    cond = jnp.concatenate(
        [c_ctx[None, :], c, jnp.zeros((N_COND_ROWS - 1 - dec_batch, d), F32)], axis=0)
    mods = _ada_call(cond, w_ada, b_ada).reshape(depth, N_COND_ROWS, 1, 6 * d)
    rope_tabs = _rope_tables(dec_seq)

    kvw = b_kv_heads * HEAD_DIM
    qw = w_in_ab.shape[2] - pool_width - 2 * kvw
    w_in_ab = jnp.concatenate(
        [w_in_ab[:, :, pool_width:pool_width + qw], w_in_ab[:, :, :pool_width],
         w_in_ab[:, :, pool_width + qw:]], axis=2)
    w_in_ab, w_out_ab, w_in_cd, w_out_cd, pool_w, w_ffn_gate, w_ffn_up, w_ffn_down = (
        w.astype(BF16) for w in (w_in_ab, w_out_ab, w_in_cd, w_out_cd, pool_w,
                                 w_ffn_gate, w_ffn_up, w_ffn_down))
    ln1_g, ln1_b, ln2_g, ln2_b, pool_scale, q_norm_g, k_norm_g = (
        v[:, None, :] for v in (ln1_g, ln1_b, ln2_g, ln2_b, pool_scale, q_norm_g, k_norm_g))

    def split_ab(p):
        return (_Cols(p, 0, qw), _Cols(p, qw, pool_width),
                _Cols(p, qw + pool_width, kvw), _Cols(p, qw + pool_width + kvw, kvw))

    def split_cd(p):
        hw = c_heads * HEAD_DIM
        return _Cols(p, 0, hw), _Cols(p, hw, hw), _Cols(p, 2 * hw, hw), _Cols(p, 3 * hw, hw)

    xp = x_prompt.reshape(batch * seq, d)
    xs = x_sample.reshape(dec_batch * dec_seq, d)
    new_state = []
    for i in range(depth):
        jj = i // 2
        ck, cv = caches[i]
        ck2 = ck.reshape(dec_batch * ck.shape[1], ck.shape[2] * HEAD_DIM).astype(BF16)
        cv2 = cv.reshape(dec_batch * cv.shape[1], cv.shape[2] * HEAD_DIM).astype(BF16)
        if i % 2 == 0:
            pp, k32, v32 = _inproj_ab_call(
                prompt, i, jj, xp, mods, w_in_ab, q_norm_g, k_norm_g, None, b_kv_heads)
            (ps,) = _inproj_ab_call(
                sample, i, jj, xs, mods, w_in_ab, q_norm_g, k_norm_g, rope_tabs, b_kv_heads, tm=1024)
            qp, up, kp, vp = split_ab(pp)
            qs, us, ks, vs = split_ab(ps)
            mix_p = (_pool_call(prompt, jj, up, pool_w, pool_scale, tr=1024),
                     _attn_seq_call(prompt, qp, kp, vp))
            mix_s = (_pool_call(sample, jj, us, pool_w, pool_scale, tr=dec_seq),
                     _attn_ctx_call(sample, qs, ks, vs, ck2, cv2))
            w_out = w_out_ab
        else:
            pp, k32, v32 = _inproj_cd_call(prompt, i, jj, xp, mods, w_in_cd, c_heads, True)
            (ps,) = _inproj_cd_call(sample, i, jj, xs, mods, w_in_cd, c_heads, False, tm=1024)
            qp, kp, vp, cp = split_cd(pp)
            qs, ks, vs, cs = split_cd(ps)
            bias = _na_bias_call(na_rpb[jj], dec_seq // GRID_W)
            mix_p = (_attn_seq_call(prompt, qp, kp, vp),
                     _conv_call(prompt, jj, cp, conv_w, tr=1024))
            mix_s = (_natten_call(sample, qs, ks, vs, ck2, cv2, bias),
                     _conv_call(sample, jj, cs, conv_w, tr=dec_seq))
            w_out = w_out_cd
        new_state += [k32, v32]
        xp = _outproj_call(prompt, i, jj, mix_p[0], mix_p[1], w_out, xp, mods, ln1_g, ln1_b, alpha)
        xs = _outproj_call(sample, i, jj, mix_s[0], mix_s[1], w_out, xs, mods, ln1_g, ln1_b, alpha)
        xp = _ffn_call(prompt, i, xp, mods, w_ffn_gate, w_ffn_up, w_ffn_down, ln2_g, ln2_b, alpha)
        xs = _ffn_call(sample, i, xs, mods, w_ffn_gate, w_ffn_up, w_ffn_down, ln2_g, ln2_b, alpha)
    return (xp.reshape(batch, seq, d), xs.reshape(dec_batch, dec_seq, d), *new_state)
```

```python
import functools
import math

import jax
import jax.numpy as jnp
from jax import lax
from jax.experimental import pallas as pl
from jax.experimental.pallas import tpu as pltpu

F32 = jnp.float32
BF16 = jnp.bfloat16

GRID_W = 64
HEAD_DIM = 128
POOL_WINDOWS = (2, 4, 8, 16)
NA_KH = 8
NA_KW = 16
ROPE_THETA = 10000.0
LN_EPS = 1e-5
RMS_EPS = 1e-6
LOG2E = math.log2(math.e)
Q_SCALE = HEAD_DIM ** -0.5 * LOG2E

NA_QROWS = 4
NA_UROWS = NA_KH + NA_QROWS
MASK_VALUE = -1e30

N_COND_ROWS = 8
N_SUB = 3
VMEM_LIMIT = 56 * 1024 * 1024


def _cparams(n_grid_dims):
    return pltpu.CompilerParams(dimension_semantics=("arbitrary",) * n_grid_dims,
                                vmem_limit_bytes=VMEM_LIMIT)


def _layer_norm(z, g, b):
    mu = jnp.mean(z, axis=-1, keepdims=True)
    zc = z - mu
    var = jnp.mean(zc * zc, axis=-1, keepdims=True)
    return zc * lax.rsqrt(var + LN_EPS) * g + b


def _dot(a, b):
    return jnp.dot(a, b, preferred_element_type=F32)


def _dot_nt(a, b):
    return lax.dot_general(a, b, (((1,), (1,)), ((), ())), preferred_element_type=F32)


def _softmax_pv(score_blocks, value_blocks):
    mx = functools.reduce(jnp.maximum, [jnp.max(s, axis=-1, keepdims=True) for s in score_blocks])
    probs = [jnp.exp2(s - mx) for s in score_blocks]
    den = sum(jnp.sum(p, axis=-1, keepdims=True) for p in probs)
    out = sum(_dot(p.astype(BF16), v) for p, v in zip(probs, value_blocks))
    return out / den


def _ada_kernel(cond_ref, w_ref, b_ref, o_ref):
    c = cond_ref[...]
    s = (c * jax.nn.sigmoid(c)).astype(BF16)
    o_ref[...] = _dot(s, w_ref[...].astype(BF16)) + b_ref[...]


def _ada_call(cond, w_ada, b_ada, tn=1024):
    depth, d, n = w_ada.shape
    return pl.pallas_call(
        _ada_kernel,
        grid=(depth, n // tn),
        in_specs=[
            pl.BlockSpec((N_COND_ROWS, d), lambda l, j: (0, 0)),
            pl.BlockSpec((None, d, tn), lambda l, j: (l, 0, j)),
            pl.BlockSpec((None, 1, tn), lambda l, j: (l, 0, j)),
        ],
        out_specs=pl.BlockSpec((None, N_COND_ROWS, tn), lambda l, j: (l, 0, j)),
        out_shape=jax.ShapeDtypeStruct((depth, N_COND_ROWS, n), F32),
        compiler_params=_cparams(2),
        name="ada",
    )(cond, w_ada, b_ada.reshape(depth, 1, n))


class _Stream:
    def __init__(self, rows, seq_len, first_cond_row, per_seq_cond):
        self.rows = rows
        self.seq_len = seq_len
        self.n_seqs = rows // seq_len
        self.first_cond_row = first_cond_row
        self.per_seq_cond = per_seq_cond

    def cond_row(self, i, tm):
        if not self.per_seq_cond:
            return self.first_cond_row
        return self.first_cond_row + (i * tm) // self.seq_len


def _mod_spec(stream, layer, chunk, tm, d, row_axis=0):
    return pl.BlockSpec((None, None, 1, d),
                        lambda *g: (layer, stream.cond_row(g[row_axis], tm), 0, chunk))


def _layer_spec(arr, layer):
    zeros = (0,) * (arr.ndim - 1)
    return pl.BlockSpec((None,) + arr.shape[1:], lambda *g: (layer,) + zeros)


class _Cols:
    def __init__(self, arr, start, width):
        self.arr, self.start, self.width = arr, start, width

    def block(self, bw):
        assert self.start % bw == 0 and self.width % bw == 0
        return self.start // bw


def _rms_heads(acc, gain, n_heads):
    outs = []
    for hh in range(n_heads):
        a = acc[:, hh * HEAD_DIM:(hh + 1) * HEAD_DIM]
        ms = jnp.mean(a * a, axis=-1, keepdims=True)
        outs.append(a * lax.rsqrt(ms + RMS_EPS) * gain)
    return outs


def _rope(x, cos, sin_signed):
    return x * cos + pltpu.roll(x, HEAD_DIM // 2, 1) * sin_signed


def _split_heads(acc):
    return [acc[:, hh * HEAD_DIM:(hh + 1) * HEAD_DIM] for hh in range(acc.shape[1] // HEAD_DIM)]


def _store_heads_4d(ref, head0, blocks, seq_len):
    for hh, val in enumerate(blocks):
        for b in range(ref.shape[0]):
            ref[b, :, head0 + hh, :] = val[b * seq_len:(b + 1) * seq_len, :]


def _store_heads(ref, head0, blocks):
    for hh, val in enumerate(blocks):
        c0 = (head0 + hh) * HEAD_DIM
        ref[:, c0:c0 + HEAD_DIM] = val.astype(BF16)


def _kv32_index(region, n_tiles):
    return lambda j, i: (jnp.where(j < region, 0, jnp.where(j == region, i, n_tiles - 1)), 0, 0, 0)


def _inproj_ab_kernel(*refs, rope, sub, seq_len):
    if rope:
        x_ref, sh_ref, sc_ref, w_ref, qg_ref, kg_ref, cos_ref, sin_ref, p_ref = refs
    else:
        x_ref, sh_ref, sc_ref, w_ref, qg_ref, kg_ref, p_ref, k32_ref, v32_ref = refs
    j = pl.program_id(0)
    heads = sub // HEAD_DIM

    def modulated():
        return (x_ref[...] * (1.0 + sc_ref[...]) + sh_ref[...]).astype(BF16)

    def sub_dot(h, s):
        return _dot(h, w_ref[:, s * sub:(s + 1) * sub])

    def normed(acc, gain):
        outs = _rms_heads(acc, gain, heads)
        if rope:
            outs = [_rope(o, cos_ref[...], sin_ref[...]) for o in outs]
        return outs

    @pl.when(j == 0)
    def _():
        h = modulated()
        for s in range(N_SUB):
            _store_heads(p_ref, s * heads, normed(sub_dot(h, s), qg_ref[...] * Q_SCALE))

    @pl.when(j == 1)
    def _():
        h = modulated()
        p_ref[:, :sub] = sub_dot(h, 0).astype(BF16)
        kn = normed(sub_dot(h, 1), kg_ref[...])
        _store_heads(p_ref, heads, kn)
        vs = _split_heads(sub_dot(h, 2))
        _store_heads(p_ref, 2 * heads, vs)
        if not rope:
            _store_heads_4d(k32_ref, 0, kn, seq_len)
            _store_heads_4d(v32_ref, 0, vs, seq_len)


def _inproj_ab_call(stream, layer, kind_layer, x, mods, w, q_gain, k_gain, rope_tabs, n_kv_heads, tm=512):
    m, d = x.shape
    n = w.shape[2]
    sub = n_kv_heads * HEAD_DIM
    n_tiles = m // tm
    assert n == 2 * N_SUB * sub
    rope = rope_tabs is not None
    assert stream.seq_len % tm == 0 if rope else tm % stream.seq_len == 0
    in_specs = [
        pl.BlockSpec((tm, d), lambda j, i: (i, 0)),
        _mod_spec(stream, layer, 0, tm, d, row_axis=1),
        _mod_spec(stream, layer, 1, tm, d, row_axis=1),
        pl.BlockSpec((None, d, N_SUB * sub), lambda j, i: (kind_layer, 0, j)),
        _layer_spec(q_gain, kind_layer),
        _layer_spec(k_gain, kind_layer),
    ]
    args = [x, mods, mods, w, q_gain, k_gain]
    out_specs = [pl.BlockSpec((tm, N_SUB * sub), lambda j, i: (i, j))]
    out_shape = [jax.ShapeDtypeStruct((m, n), BF16)]
    if rope:
        tiles_per_seq = stream.seq_len // tm
        tab_spec = pl.BlockSpec((tm, HEAD_DIM), lambda j, i: (i % tiles_per_seq, 0))
        in_specs += [tab_spec, tab_spec]
        args += list(rope_tabs)
    else:
        kv_block = (tm // stream.seq_len, stream.seq_len, n_kv_heads, HEAD_DIM)
        out_specs += [pl.BlockSpec(kv_block, _kv32_index(1, n_tiles))] * 2
        out_shape += [jax.ShapeDtypeStruct((stream.n_seqs,) + kv_block[1:], F32)] * 2
    return pl.pallas_call(
        functools.partial(_inproj_ab_kernel, rope=rope, sub=sub, seq_len=stream.seq_len),
        grid=(2, n_tiles),
        in_specs=in_specs,
        out_specs=tuple(out_specs),
        out_shape=tuple(out_shape),
        compiler_params=_cparams(2),
        name="inproj_ab",
    )(*args)


def _inproj_cd_kernel(*refs, emit_kv, seq_len):
    if emit_kv:
        x_ref, sh_ref, sc_ref, w_ref, p_ref, k32_ref, v32_ref = refs
    else:
        x_ref, sh_ref, sc_ref, w_ref, p_ref = refs
    j = pl.program_id(0)
    sub = w_ref.shape[1] // N_SUB
    heads = sub // HEAD_DIM

    def region(scale, out32_ref):
        h = (x_ref[...] * (1.0 + sc_ref[...]) + sh_ref[...]).astype(BF16)
        for s in range(N_SUB):
            acc = _dot(h, w_ref[:, s * sub:(s + 1) * sub])
            if scale is not None:
                acc = acc * scale
            p_ref[:, s * sub:(s + 1) * sub] = acc.astype(BF16)
            if out32_ref is not None:
                _store_heads_4d(out32_ref, s * heads, _split_heads(acc), seq_len)

    @pl.when(j == 0)
    def _():
        region(Q_SCALE, None)

    @pl.when(j == 1)
    def _():
        region(None, k32_ref if emit_kv else None)

    @pl.when(j == 2)
    def _():
        region(None, v32_ref if emit_kv else None)

    @pl.when(j == 3)
    def _():
        region(None, None)


def _inproj_cd_call(stream, layer, kind_layer, x, mods, w, n_heads, emit_kv, tm=512):
    m, d = x.shape
    n = w.shape[2]
    hw = n_heads * HEAD_DIM
    n_tiles = m // tm
    assert n == 4 * hw and n_heads % N_SUB == 0
    in_specs = [
        pl.BlockSpec((tm, d), lambda j, i: (i, 0)),
        _mod_spec(stream, layer, 0, tm, d, row_axis=1),
        _mod_spec(stream, layer, 1, tm, d, row_axis=1),
        pl.BlockSpec((None, d, hw), lambda j, i: (kind_layer, 0, j)),
    ]
    out_specs = [pl.BlockSpec((tm, hw), lambda j, i: (i, j))]
    out_shape = [jax.ShapeDtypeStruct((m, n), BF16)]
    if emit_kv:
        kv_block = (tm // stream.seq_len, stream.seq_len, n_heads, HEAD_DIM)
        out_specs += [pl.BlockSpec(kv_block, _kv32_index(1, n_tiles)),
                      pl.BlockSpec(kv_block, _kv32_index(2, n_tiles))]
        out_shape += [jax.ShapeDtypeStruct((stream.n_seqs,) + kv_block[1:], F32)] * 2
    return pl.pallas_call(
        functools.partial(_inproj_cd_kernel, emit_kv=emit_kv, seq_len=stream.seq_len),
        grid=(4, n_tiles),
        in_specs=in_specs,
        out_specs=tuple(out_specs),
        out_shape=tuple(out_shape),
        compiler_params=_cparams(2),
        name="inproj_cd",
    )(x, mods, mods, w)


def _pool_kernel(u_ref, pw_ref, ps_ref, o_ref, *, seq_len):
    rows = u_ref.shape[0]
    cw = pw_ref.shape[1]
    t = lax.broadcasted_iota(jnp.int32, (rows, cw), 0) & (seq_len - 1)
    for g, w in enumerate(POOL_WINDOWS):
        sl = slice(g * cw, (g + 1) * cw)
        x = u_ref[:, sl].astype(F32)
        acc = jnp.zeros_like(x)
        for dlt in range(-(w // 2), w - w // 2):
            xs = x if dlt == 0 else pltpu.roll(x, (-dlt) % rows, 0)
            valid = (t + dlt >= 0) & (t + dlt <= seq_len - 1)
            acc = acc + jnp.where(valid, xs, 0.0)
        lo = jnp.clip(t - w // 2, 0, seq_len - 1)
        hi = jnp.clip(t - w // 2 + w - 1, 0, seq_len - 1)
        cnt = (hi - lo + 1).astype(F32)
        pooled = (acc / cnt - x).astype(BF16)
        y = _dot(pooled, pw_ref[g]) * ps_ref[:, sl]
        o_ref[:, sl] = y.astype(BF16)


def _pool_call(stream, kind_layer, u, pool_w, pool_scale, tr):
    m, pw = u.arr.shape[0], u.width
    ub = u.block(pw)
    assert tr % stream.seq_len == 0
    return pl.pallas_call(
        functools.partial(_pool_kernel, seq_len=stream.seq_len),
        grid=(m // tr,),
        in_specs=[
            pl.BlockSpec((tr, pw), lambda i: (i, ub)),
            _layer_spec(pool_w, kind_layer),
            _layer_spec(pool_scale, kind_layer),
        ],
        out_specs=pl.BlockSpec((tr, pw), lambda i: (i, 0)),
        out_shape=jax.ShapeDtypeStruct((m, pw), BF16),
        compiler_params=_cparams(1),
        name="pool",
    )(u.arr, pool_w, pool_scale)


def _conv_kernel(xin_ref, gb_ref, gc_ref, cw_ref, o_ref, *, seq_len):
    rows, width = xin_ref.shape
    t = lax.broadcasted_iota(jnp.int32, (rows, width), 0) & (seq_len - 1)
    u = gc_ref[...].astype(F32) * xin_ref[...].astype(F32)
    prev = jnp.where(t >= 1, pltpu.roll(u, 1, 0), 0.0)
    nxt = jnp.where(t <= seq_len - 2, pltpu.roll(u, rows - 1, 0), 0.0)
    y = cw_ref[0:1, :] * prev + cw_ref[1:2, :] * u + cw_ref[2:3, :] * nxt
    o_ref[...] = (gb_ref[...].astype(F32) * y).astype(BF16)


def _conv_call(stream, kind_layer, cin, conv_w, tr):
    m = cin.arr.shape[0]
    width = conv_w.shape[2]
    cb = cin.block(width)
    assert tr % stream.seq_len == 0 and cin.width == 3 * width
    return pl.pallas_call(
        functools.partial(_conv_kernel, seq_len=stream.seq_len),
        grid=(m // tr,),
        in_specs=[
            pl.BlockSpec((tr, width), lambda i: (i, cb)),
            pl.BlockSpec((tr, width), lambda i: (i, cb + 1)),
            pl.BlockSpec((tr, width), lambda i: (i, cb + 2)),
            _layer_spec(conv_w, kind_layer),
        ],
        out_specs=pl.BlockSpec((tr, width), lambda i: (i, 0)),
        out_shape=jax.ShapeDtypeStruct((m, width), BF16),
        compiler_params=_cparams(1),
        name="conv",
    )(cin.arr, cin.arr, cin.arr, conv_w)


def _head(ref, h, rows=slice(None)):
    return ref[rows, h * HEAD_DIM:(h + 1) * HEAD_DIM]


def _attn_seq_kernel(q_ref, k_ref, v_ref, o_ref, *, n_q_heads, group, seq_len):
    for sq in range(q_ref.shape[0] // seq_len):
        rows = slice(sq * seq_len, (sq + 1) * seq_len)
        for h in range(n_q_heads):
            kv = h // group
            s = _dot_nt(_head(q_ref, h, rows), _head(k_ref, kv, rows))
            o = _softmax_pv([s], [_head(v_ref, kv, rows)])
            o_ref[rows, h * HEAD_DIM:(h + 1) * HEAD_DIM] = o.astype(BF16)


def _attn_seq_call(stream, q, k, v, seqs_per_step=4):
    m = q.arr.shape[0]
    qw, kw = q.width, k.width
    qb, kb, vb = q.block(qw), k.block(kw), v.block(kw)
    tr = seqs_per_step * stream.seq_len
    return pl.pallas_call(
        functools.partial(_attn_seq_kernel, n_q_heads=qw // HEAD_DIM, group=qw // kw, seq_len=stream.seq_len),
        grid=(m // tr,),
        in_specs=[
            pl.BlockSpec((tr, qw), lambda b: (b, qb)),
            pl.BlockSpec((tr, kw), lambda b: (b, kb)),
            pl.BlockSpec((tr, kw), lambda b: (b, vb)),
        ],
        out_specs=pl.BlockSpec((tr, qw), lambda b: (b, 0)),
        out_shape=jax.ShapeDtypeStruct((m, qw), BF16),
        compiler_params=_cparams(1),
        name="attn_seq",
    )(q.arr, k.arr, v.arr)


def _attn_ctx_kernel(q_ref, k_ref, v_ref, kc_ref, vc_ref, o_ref):
    for g in range(q_ref.shape[1] // HEAD_DIM):
        q = _head(q_ref, g)
        o = _softmax_pv([_dot_nt(q, k_ref[...]), _dot_nt(q, kc_ref[...])], [v_ref[...], vc_ref[...]])
        o_ref[:, g * HEAD_DIM:(g + 1) * HEAD_DIM] = o.astype(BF16)


def _attn_ctx_call(stream, q, k, v, ctx_k, ctx_v, tq=1024):
    m = q.arr.shape[0]
    qw = q.width
    n_kv_heads = k.width // HEAD_DIM
    gw = qw // n_kv_heads
    qb, kb, vb = q.block(gw), k.block(HEAD_DIM), v.block(HEAD_DIM)
    ls = stream.seq_len
    nb = stream.n_seqs
    nq = ls // tq
    past = ctx_k.shape[0] // nb
    return pl.pallas_call(
        _attn_ctx_kernel,
        grid=(nb, n_kv_heads, nq),
        in_specs=[
            pl.BlockSpec((tq, gw), lambda b, h, qi: (b * nq + qi, qb + h)),
            pl.BlockSpec((ls, HEAD_DIM), lambda b, h, qi: (b, kb + h)),
            pl.BlockSpec((ls, HEAD_DIM), lambda b, h, qi: (b, vb + h)),
            pl.BlockSpec((past, HEAD_DIM), lambda b, h, qi: (b, h)),
            pl.BlockSpec((past, HEAD_DIM), lambda b, h, qi: (b, h)),
        ],
        out_specs=pl.BlockSpec((tq, gw), lambda b, h, qi: (b * nq + qi, h)),
        out_shape=jax.ShapeDtypeStruct((m, qw), BF16),
        compiler_params=_cparams(3),
        name="attn_ctx",
    )(q.arr, k.arr, v.arr, ctx_k, ctx_v)


def _na_bias_kernel(rpb_ref, o_ref, *, n_rows):
    h = pl.program_id(0)
    n_dr = 2 * NA_KH - 1
    n_dc = 2 * NA_KW - 1
    pair_w = 2 * GRID_W
    lane = lax.broadcasted_iota(jnp.int32, (GRID_W, pair_w), 1)
    qcol = lax.broadcasted_iota(jnp.int32, (GRID_W, pair_w), 0)
    kcol = lane & (GRID_W - 1)
    second = lane >= GRID_W
    col_start = jnp.clip(qcol - NA_KW // 2, 0, GRID_W - NA_KW)
    col_ok = (kcol >= col_start) & (kcol < col_start + NA_KW)
    dc_idx = jnp.clip(kcol - qcol, -(NA_KW - 1), NA_KW - 1) + NA_KW - 1

    pair_tiles = []
    for dr in range(-1, n_dr):
        dr_a = min(max(dr, 0), n_dr - 1)
        dr_b = min(max(dr + 1, 0), n_dr - 1)
        tile = jnp.zeros((GRID_W, pair_w), F32)
        for dc in range(n_dc):
            base = h * (n_dr * n_dc) + dc
            val = jnp.where(second, rpb_ref[base + dr_b * n_dc], rpb_ref[base + dr_a * n_dc])
            tile = jnp.where(dc_idx == dc, val, tile)
        pair_tiles.append(jnp.where(col_ok, tile * LOG2E, MASK_VALUE))

    krow = lax.broadcasted_iota(jnp.int32, (GRID_W, NA_UROWS * GRID_W), 1) >> (GRID_W.bit_length() - 1)
    n_blocks = n_rows // NA_QROWS
    for pat, blk in enumerate((0, 1, n_blocks - 1)):
        r0 = blk * NA_QROWS
        us = min(max(r0 - NA_KH // 2, 0), n_rows - NA_UROWS)
        for i in range(NA_QROWS):
            r = r0 + i
            rs = min(max(r - NA_KH // 2, 0), n_rows - NA_KH)
            pieces = []
            for jp in range(NA_UROWS // 2):
                dr = (us + 2 * jp) - r + NA_KH - 1
                pieces.append(pair_tiles[min(max(dr, -1), n_dr - 1) + 1])
            strip = jnp.concatenate(pieces, axis=1)
            in_window = (krow >= rs - us) & (krow < rs - us + NA_KH)
            o_ref[pat, i * GRID_W:(i + 1) * GRID_W, :] = jnp.where(in_window, strip, MASK_VALUE)


def _na_bias_call(rpb, n_rows):
    heads = rpb.shape[0]
    assert NA_UROWS % 2 == 0 and n_rows // NA_QROWS >= 3
    q_tok, k_tok = NA_QROWS * GRID_W, NA_UROWS * GRID_W
    return pl.pallas_call(
        functools.partial(_na_bias_kernel, n_rows=n_rows),
        grid=(heads,),
        in_specs=[pl.BlockSpec(memory_space=pltpu.SMEM)],
        out_specs=pl.BlockSpec((3, None, q_tok, k_tok), lambda h: (0, h, 0, 0)),
        out_shape=jax.ShapeDtypeStruct((3, heads, q_tok, k_tok), F32),
        compiler_params=_cparams(1),
        name="na_bias",
    )(rpb.reshape(-1))


def _natten_kernel(q_ref, k_ref, v_ref, kc_ref, vc_ref, bias_ref, o_ref, *, n_rows):
    rb = pl.program_id(2)
    us = jnp.clip(rb * NA_QROWS - NA_KH // 2, 0, n_rows - NA_UROWS)
    start = pl.multiple_of(us * GRID_W, GRID_W * NA_QROWS)
    band = pl.ds(start, NA_UROWS * GRID_W)
    for h in range(q_ref.shape[1] // HEAD_DIM):
        q = _head(q_ref, h)
        s_nb = _dot_nt(q, _head(k_ref, h, band)) + bias_ref[h]
        s_ctx = _dot_nt(q, _head(kc_ref, h))
        o = _softmax_pv([s_nb, s_ctx], [_head(v_ref, h, band), _head(vc_ref, h)])
        o_ref[:, h * HEAD_DIM:(h + 1) * HEAD_DIM] = o.astype(BF16)


def _natten_call(stream, q, k, v, ctx_k, ctx_v, bias, heads_per_step=6):
    m, hw = q.arr.shape[0], q.width
    ls = stream.seq_len
    nb = stream.n_seqs
    n_rows = ls // GRID_W
    n_blocks = n_rows // NA_QROWS
    tq = NA_QROWS * GRID_W
    past = ctx_k.shape[0] // nb
    gw = heads_per_step * HEAD_DIM
    qb, kb, vb = q.block(gw), k.block(gw), v.block(gw)

    def pattern(rb):
        return jnp.where(rb == 0, 0, jnp.where(rb == n_blocks - 1, 2, 1))

    return pl.pallas_call(
        functools.partial(_natten_kernel, n_rows=n_rows),
        grid=(nb, hw // gw, n_blocks),
        in_specs=[
            pl.BlockSpec((tq, gw), lambda b, h, rb: (b * n_blocks + rb, qb + h)),
            pl.BlockSpec((ls, gw), lambda b, h, rb: (b, kb + h)),
            pl.BlockSpec((ls, gw), lambda b, h, rb: (b, vb + h)),
            pl.BlockSpec((past, gw), lambda b, h, rb: (b, h)),
            pl.BlockSpec((past, gw), lambda b, h, rb: (b, h)),
            pl.BlockSpec((None, heads_per_step, tq, NA_UROWS * GRID_W),
                         lambda b, h, rb: (pattern(rb), h, 0, 0)),
        ],
        out_specs=pl.BlockSpec((tq, gw), lambda b, h, rb: (b * n_blocks + rb, h)),
        out_shape=jax.ShapeDtypeStruct((m, hw), BF16),
        compiler_params=_cparams(3),
        name="natten",
    )(q.arr, k.arr, v.arr, ctx_k, ctx_v, bias)


def _outproj_kernel(a_ref, b_ref, w_ref, x_ref, gate_ref, g_ref, beta_ref, o_ref, *, alpha, n_chunks):
    ka = a_ref.shape[1]
    rc = x_ref.shape[0] // n_chunks
    for c in range(n_chunks):
        rows = slice(c * rc, (c + 1) * rc)
        y = _dot(a_ref[rows, :], w_ref[:ka, :]) + _dot(b_ref[rows, :], w_ref[ka:, :])
        o_ref[rows, :] = _layer_norm(alpha * x_ref[rows, :] + gate_ref[...] * y, g_ref[...], beta_ref[...])


def _outproj_call(stream, layer, kind_layer, a, b, w, x, mods, ln_g, ln_b, alpha, tm=512, n_chunks=2):
    m, d = x.shape
    ka, kb = a.shape[1], b.shape[1]
    return pl.pallas_call(
        functools.partial(_outproj_kernel, alpha=alpha, n_chunks=n_chunks),
        grid=(m // tm,),
        in_specs=[
            pl.BlockSpec((tm, ka), lambda i: (i, 0)),
            pl.BlockSpec((tm, kb), lambda i: (i, 0)),
            _layer_spec(w, kind_layer),
            pl.BlockSpec((tm, d), lambda i: (i, 0)),
            _mod_spec(stream, layer, 2, tm, d),
            _layer_spec(ln_g, layer),
            _layer_spec(ln_b, layer),
        ],
        out_specs=pl.BlockSpec((tm, d), lambda i: (i, 0)),
        out_shape=jax.ShapeDtypeStruct((m, d), F32),
        compiler_params=_cparams(1),
        name="outproj_ln",
    )(a, b, w, x, mods, ln_g, ln_b)


def _ffn_kernel(x_ref, sh_ref, sc_ref, gate_ref, wg_ref, wu_ref, wd_ref, g_ref, beta_ref, o_ref,
                h_scr, *, alpha, n_chunks):
    j = pl.program_id(1)
    last = pl.num_programs(1) - 1
    rc = x_ref.shape[0] // n_chunks
    chunks = [slice(c * rc, (c + 1) * rc) for c in range(n_chunks)]

    def hidden_chunk(h):
        gt = _dot(h, wg_ref[...])
        up = _dot(h, wu_ref[...])
        act = (gt * jax.nn.sigmoid(gt) * up).astype(BF16)
        return _dot(act, wd_ref[...])

    @pl.when(j == 0)
    def _():
        for rows in chunks:
            h = (x_ref[rows, :] * (1.0 + sc_ref[...]) + sh_ref[...]).astype(BF16)
            h_scr[rows, :] = h
            o_ref[rows, :] = hidden_chunk(h)

    @pl.when((j > 0) & (j < last))
    def _():
        for rows in chunks:
            o_ref[rows, :] += hidden_chunk(h_scr[rows, :])

    @pl.when(j == last)
    def _():
        for rows in chunks:
            y = o_ref[rows, :] + hidden_chunk(h_scr[rows, :])
            o_ref[rows, :] = _layer_norm(alpha * x_ref[rows, :] + gate_ref[...] * y,
                                         g_ref[...], beta_ref[...])


def _ffn_call(stream, layer, x, mods, wg, wu, wd, ln_g, ln_b, alpha, tm=1024, tf=512, n_chunks=2):
    m, d = x.shape
    f = wg.shape[2]
    return pl.pallas_call(
        functools.partial(_ffn_kernel, alpha=alpha, n_chunks=n_chunks),
        grid=(m // tm, f // tf),
        in_specs=[
            pl.BlockSpec((tm, d), lambda i, j: (i, 0)),
            _mod_spec(stream, layer, 3, tm, d),
            _mod_spec(stream, layer, 4, tm, d),
            _mod_spec(stream, layer, 5, tm, d),
            pl.BlockSpec((None, d, tf), lambda i, j: (layer, 0, j)),
            pl.BlockSpec((None, d, tf), lambda i, j: (layer, 0, j)),
            pl.BlockSpec((None, tf, d), lambda i, j: (layer, j, 0)),
            _layer_spec(ln_g, layer),
            _layer_spec(ln_b, layer),
        ],
        out_specs=pl.BlockSpec((tm, d), lambda i, j: (i, 0)),
        out_shape=jax.ShapeDtypeStruct((m, d), F32),
        scratch_shapes=[pltpu.VMEM((tm, d), BF16)],
        compiler_params=_cparams(2),
        name="ffn_ln",
    )(x, mods, mods, mods, wg, wu, wd, ln_g, ln_b)


def _rope_tables(n_tokens):
    t = jnp.arange(n_tokens)
    row = (t // GRID_W).astype(F32)
    col = (t % GRID_W).astype(F32)
    half = HEAD_DIM // 2
    inv = 1.0 / (ROPE_THETA ** (jnp.arange(0, half, 2, dtype=F32) / half))
    ang = jnp.concatenate([row[:, None] * inv, col[:, None] * inv], axis=-1)
    cos, sin = jnp.cos(ang), jnp.sin(ang)
    return jnp.concatenate([cos, cos], axis=-1), jnp.concatenate([-sin, sin], axis=-1)


def kernel(x_prompt, x_sample, cache_k_l0, cache_v_l0, cache_k_l1, cache_v_l1, cache_k_l2, cache_v_l2, cache_k_l3, cache_v_l3, c, c_ctx, w_ada, b_ada, ln1_g, ln1_b, ln2_g, ln2_b, w_in_ab, w_out_ab, pool_w, pool_scale, q_norm_g, k_norm_g, w_in_cd, w_out_cd, na_rpb, conv_w, w_ffn_gate, w_ffn_up, w_ffn_down):
    batch, seq, d = x_prompt.shape
    dec_batch, dec_seq, _ = x_sample.shape
    depth = w_ada.shape[0]
    alpha = (2 * depth) ** 0.25
    b_kv_heads = cache_k_l0.shape[2]
    c_heads = cache_k_l1.shape[2]
    pool_width = pool_w.shape[1] * pool_w.shape[2]
    caches = [(cache_k_l0, cache_v_l0), (cache_k_l1, cache_v_l1),
              (cache_k_l2, cache_v_l2), (cache_k_l3, cache_v_l3)]

    prompt = _Stream(batch * seq, seq, 0, False)
    sample = _Stream(dec_batch * dec_seq, dec_seq, 1, True)
    assert 1 + dec_batch <= N_COND_ROWS

    cond = jnp.concatenate(
        [c_ctx[None, :], c, jnp.zeros((N_COND_ROWS - 1 - dec_batch, d), F32)], axis=0)
    mods = _ada_call(cond, w_ada, b_ada).reshape(depth, N_COND_ROWS, 1, 6 * d)
    rope_tabs = _rope_tables(dec_seq)

    kvw = b_kv_heads * HEAD_DIM
    qw = w_in_ab.shape[2] - pool_width - 2 * kvw
    w_in_ab = jnp.concatenate(
        [w_in_ab[:, :, pool_width:pool_width + qw], w_in_ab[:, :, :pool_width],
         w_in_ab[:, :, pool_width + qw:]], axis=2)
    w_in_ab, w_out_ab, w_in_cd, w_out_cd, pool_w, w_ffn_gate, w_ffn_up, w_ffn_down = (
        w.astype(BF16) for w in (w_in_ab, w_out_ab, w_in_cd, w_out_cd, pool_w,
                                 w_ffn_gate, w_ffn_up, w_ffn_down))
    ln1_g, ln1_b, ln2_g, ln2_b, pool_scale, q_norm_g, k_norm_g = (
        v[:, None, :] for v in (ln1_g, ln1_b, ln2_g, ln2_b, pool_scale, q_norm_g, k_norm_g))

    def split_ab(p):
        return (_Cols(p, 0, qw), _Cols(p, qw, pool_width),
                _Cols(p, qw + pool_width, kvw), _Cols(p, qw + pool_width + kvw, kvw))

    def split_cd(p):
        hw = c_heads * HEAD_DIM
        return _Cols(p, 0, hw), _Cols(p, hw, hw), _Cols(p, 2 * hw, hw), _Cols(p, 3 * hw, hw)

    xp = x_prompt.reshape(batch * seq, d)
    xs = x_sample.reshape(dec_batch * dec_seq, d)
    new_state = []
    for i in range(depth):
        jj = i // 2
        ck, cv = caches[i]
        ck2 = ck.reshape(dec_batch * ck.shape[1], ck.shape[2] * HEAD_DIM).astype(BF16)
        cv2 = cv.reshape(dec_batch * cv.shape[1], cv.shape[2] * HEAD_DIM).astype(BF16)
        if i % 2 == 0:
            pp, k32, v32 = _inproj_ab_call(
                prompt, i, jj, xp, mods, w_in_ab, q_norm_g, k_norm_g, None, b_kv_heads, tm=1024)
            (ps,) = _inproj_ab_call(
                sample, i, jj, xs, mods, w_in_ab, q_norm_g, k_norm_g, rope_tabs, b_kv_heads, tm=1024)
            qp, up, kp, vp = split_ab(pp)
            qs, us, ks, vs = split_ab(ps)
            mix_p = (_pool_call(prompt, jj, up, pool_w, pool_scale, tr=1024),
                     _attn_seq_call(prompt, qp, kp, vp))
            mix_s = (_pool_call(sample, jj, us, pool_w, pool_scale, tr=dec_seq),
                     _attn_ctx_call(sample, qs, ks, vs, ck2, cv2))
            w_out = w_out_ab
        else:
            pp, k32, v32 = _inproj_cd_call(prompt, i, jj, xp, mods, w_in_cd, c_heads, True)
            (ps,) = _inproj_cd_call(sample, i, jj, xs, mods, w_in_cd, c_heads, False, tm=1024)
            qp, kp, vp, cp = split_cd(pp)
            qs, ks, vs, cs = split_cd(ps)
            bias = _na_bias_call(na_rpb[jj], dec_seq // GRID_W)
            mix_p = (_attn_seq_call(prompt, qp, kp, vp),
                     _conv_call(prompt, jj, cp, conv_w, tr=1024))
            mix_s = (_natten_call(sample, qs, ks, vs, ck2, cv2, bias),
                     _conv_call(sample, jj, cs, conv_w, tr=dec_seq))
            w_out = w_out_cd
        new_state += [k32, v32]
        xp = _outproj_call(prompt, i, jj, mix_p[0], mix_p[1], w_out, xp, mods, ln1_g, ln1_b, alpha)
        xs = _outproj_call(sample, i, jj, mix_s[0], mix_s[1], w_out, xs, mods, ln1_g, ln1_b, alpha)
        xp = _ffn_call(prompt, i, xp, mods, w_ffn_gate, w_ffn_up, w_ffn_down, ln2_g, ln2_b, alpha)
        xs = _ffn_call(sample, i, xs, mods, w_ffn_gate, w_ffn_up, w_ffn_down, ln2_g, ln2_b, alpha)
    return (xp.reshape(batch, seq, d), xs.reshape(dec_batch, dec_seq, d), *new_state)
```
